```python
import jax, jax.numpy as jnp
from jax import lax
import numpy as np

D_MODEL = 1024
BATCH = 8
SEQ = 2048
DEPTH = 2
DEC_BATCH = 32
DEC_SEQ = 4
PAST_LEN = 8192
PAGE_SIZE = 128

N_EVEN = (DEPTH + 1) // 2
N_ODD = DEPTH // 2
EPS = 1e-6
D_A = D_MODEL // 2
CONV_A_WIDTH = 31
GDN_HEADS = 4
GDN_DK = 128
GDN_DV = 128
GDN_CONV = 4
GDN_CHUNK = 64
QKV_W = GDN_HEADS * (2 * GDN_DK + GDN_DV)
IN_AB = 2 * D_A + QKV_W + GDN_HEADS * GDN_DV + 2 * GDN_HEADS
C_PATTERNS = ((128, 1), (512, 4), (2048, 16))
N_GROUPS = len(C_PATTERNS)
C_HEADS = 8
C_HD = D_MODEL // C_HEADS
ROT_DIM = C_HD // 4
ROPE_THETA = 500000.0
ATT_BLOCK = 128
IN_C = 3 * N_GROUPS * C_HEADS * C_HD
PEER_HEADS = 8
N_KEYS = 128
N_EXPERTS = N_KEYS * N_KEYS
PEER_TOPK = 16
PEER_DKEY = 128
PEER_HALF = PEER_DKEY // 2
PEER_BLOCK = 128
PLE_DIM = 256

kernel_name = 'hybrid_conv_delta_dilated_peer_decode_step'


def _rmsnorm(x, g):
    xf = x.astype(jnp.float32)
    y = xf * lax.rsqrt(jnp.mean(xf * xf, axis=-1, keepdims=True) + EPS)
    return (y * g.astype(jnp.float32)).astype(x.dtype)


def _layernorm(x, g, b):
    xf = x.astype(jnp.float32)
    xc = xf - jnp.mean(xf, axis=-1, keepdims=True)
    y = xc * lax.rsqrt(jnp.mean(xc * xc, axis=-1, keepdims=True) + EPS)
    return (y * g.astype(jnp.float32) + b.astype(jnp.float32)).astype(x.dtype)


def _l2norm(x):
    xf = x.astype(jnp.float32)
    return xf * lax.rsqrt(jnp.sum(xf * xf, axis=-1, keepdims=True) + EPS)


def _causal_dwconv(x, buf, w):
    xx = jnp.concatenate([buf.astype(x.dtype), x], axis=1)
    y = lax.conv_general_dilated(xx, w.astype(x.dtype)[:, None, :], window_strides=(1,), padding='VALID',
                                 dimension_numbers=('NWC', 'WIO', 'NWC'), feature_group_count=x.shape[-1])
    return y, xx[:, xx.shape[1] - (w.shape[0] - 1):]


def _gated_delta(q, k, v, g, beta, s0):
    B, T, H, K = q.shape
    V = v.shape[-1]
    C = GDN_CHUNK if T % GDN_CHUNK == 0 else T
    N = T // C

    def chunks(t):
        return t.reshape(B, N, C, H, -1).transpose(1, 0, 3, 2, 4)

    qc, kc, vc = chunks(q), chunks(k), chunks(v.astype(jnp.float32))
    gc = chunks(g[..., None])[..., 0].astype(jnp.float32)
    bc = chunks(beta[..., None])[..., 0].astype(jnp.float32)
    gcum = jnp.cumsum(gc, axis=-1)
    causal = jnp.tril(jnp.ones((C, C), dtype=bool))
    strict = jnp.tril(jnp.ones((C, C), dtype=bool), -1)
    decay = jnp.exp(jnp.where(causal, gcum[..., :, None] - gcum[..., None, :], -jnp.inf))
    kb = kc * bc[..., None]
    lmat = jnp.where(strict, jnp.einsum('nbhik,nbhjk->nbhij', kb, kc) * decay, 0.0)
    amat = lmat + jnp.eye(C, dtype=jnp.float32)
    rhs = jnp.concatenate([vc * bc[..., None], kb * jnp.exp(gcum)[..., None]], axis=-1)
    sol = lax.linalg.triangular_solve(amat, rhs, left_side=True, lower=True, unit_diagonal=True)
    u, w = sol[..., :V], sol[..., V:]
    intra = jnp.where(causal, jnp.einsum('nbhik,nbhjk->nbhij', qc, kc) * decay, 0.0)

    def step(S, inp):
        qi, ki, ui, wi, gi, ai = inp
        vnew = ui - jnp.einsum('bhck,bhkv->bhcv', wi, S)
        o = (jnp.einsum('bhck,bhkv->bhcv', qi * jnp.exp(gi)[..., None], S)
             + jnp.einsum('bhij,bhjv->bhiv', ai, vnew))
        glast = gi[..., -1:]
        S = (S * jnp.exp(glast)[..., None]
             + jnp.einsum('bhck,bhcv->bhkv', ki * jnp.exp(glast - gi)[..., None], vnew))
        return S, o

    s_fin, o = lax.scan(step, s0.astype(jnp.float32), (qc, kc, u, w, gcum, intra))
    return o.transpose(1, 0, 3, 2, 4).reshape(B, T, H, V), s_fin


def _mixer_conv_delta(h, buf_a, buf_qkv, s0, w_in, conv_a_w, conv_a_b, ln_a_g, ln_a_b, conv_qkv_w,
                      a_log, dt_bias, gdn_norm_g, w_out):
    B, T, _ = h.shape
    z = h @ w_in
    o1 = 2 * D_A
    o2 = o1 + QKV_W
    o3 = o2 + GDN_HEADS * GDN_DV
    o4 = o3 + GDN_HEADS
    glu = z[..., :D_A] * jax.nn.sigmoid(z[..., D_A:o1])
    ca, new_buf_a = _causal_dwconv(glu, buf_a, conv_a_w)
    ya = jax.nn.silu(_layernorm(ca + conv_a_b, ln_a_g, ln_a_b))
    qkv, new_buf_qkv = _causal_dwconv(z[..., o1:o2], buf_qkv, conv_qkv_w)
    qkv = jax.nn.silu(qkv)
    nq = GDN_HEADS * GDN_DK
    q = _l2norm(qkv[..., :nq].reshape(B, T, GDN_HEADS, GDN_DK)) * (GDN_DK ** -0.5)
    k = _l2norm(qkv[..., nq:2 * nq].reshape(B, T, GDN_HEADS, GDN_DK))
    v = qkv[..., 2 * nq:].reshape(B, T, GDN_HEADS, GDN_DV)
    gate = z[..., o2:o3].reshape(B, T, GDN_HEADS, GDN_DV)
    beta = jax.nn.sigmoid(z[..., o3:o4].astype(jnp.float32))
    g = -jnp.exp(a_log.astype(jnp.float32)) * jax.nn.softplus(z[..., o4:].astype(jnp.float32) + dt_bias.astype(jnp.float32))
    od, s_new = _gated_delta(q, k, v, g, beta, s0)
    od = _rmsnorm(od, gdn_norm_g) * jax.nn.silu(gate.astype(jnp.float32))
    mixed = jnp.concatenate([ya.astype(h.dtype), od.reshape(B, T, GDN_HEADS * GDN_DV).astype(h.dtype)], axis=-1)
    return mixed @ w_out, new_buf_a, new_buf_qkv, s_new


def _partial_rope(x, pos):
    half = ROT_DIM // 2
    inv = 1.0 / (ROPE_THETA ** (jnp.arange(0, ROT_DIM, 2, dtype=jnp.float32) / ROT_DIM))
    ang = pos.astype(jnp.float32)[:, None] * inv[None, :]
    cos = jnp.cos(ang)[None, :, None, None, :]
    sin = jnp.sin(ang)[None, :, None, None, :]
    xf = x.astype(jnp.float32)
    x1, x2 = xf[..., :half], xf[..., half:ROT_DIM]
    return jnp.concatenate([x1 * cos - x2 * sin, x2 * cos + x1 * sin, xf[..., ROT_DIM:]], axis=-1).astype(x.dtype)


def _c_qkv(h, pos, w_in):
    B, T, _ = h.shape
    qkv = (h @ w_in).reshape(B, T, 3, N_GROUPS, C_HEADS, C_HD)
    return _partial_rope(qkv[:, :, 0], pos), _partial_rope(qkv[:, :, 1], pos), qkv[:, :, 2]


def _dilated_band_attn(q, k, v, dil, band):
    B, S, H, E = q.shape
    n = S // dil
    nb = -(-n // ATT_BLOCK)
    pad = nb * ATT_BLOCK - n

    def split(t):
        t = t.astype(jnp.float32).reshape(B, n, dil, H, E).transpose(0, 2, 1, 3, 4)
        t = jnp.pad(t, ((0, 0), (0, 0), (0, pad), (0, 0), (0, 0)))
        return t.reshape(B, dil, nb, ATT_BLOCK, H, E)

    def with_prev(t):
        prev = jnp.pad(t, ((0, 0), (0, 0), (1, 0), (0, 0), (0, 0), (0, 0)))[:, :, :nb]
        return jnp.concatenate([prev, t], axis=3)

    qb = split(q)
    kk, vv = with_prev(split(k)), with_prev(split(v))
    s = jnp.einsum('brnqhe,brnkhe->brnqhk', qb, kk) * (E ** -0.5)
    qi = jnp.arange(ATT_BLOCK)[:, None]
    ki = jnp.arange(2 * ATT_BLOCK)[None, :]
    rel = ATT_BLOCK + qi - ki
    in_band = (rel >= 0) & (rel <= band)
    has_prev = (jnp.arange(nb)[:, None, None] > 0) | (ki[None] >= ATT_BLOCK)
    mask = (in_band[None] & has_prev)[:, :, None, :]
    s = jnp.where(mask, s, -jnp.inf)
    m = jnp.max(s, axis=-1, keepdims=True)
    p = jnp.exp(s - m)
    den = jnp.sum(p, axis=-1, keepdims=True)
    o = jnp.einsum('brnqhk,brnkhe->brnqhe', p, vv) / den
    lse = (m + jnp.log(den))[..., 0]
    o = o.reshape(B, dil, nb * ATT_BLOCK, H, E)[:, :, :n].transpose(0, 2, 1, 3, 4).reshape(B, S, H, E)
    lse = lse.reshape(B, dil, nb * ATT_BLOCK, H)[:, :, :n].transpose(0, 2, 1, 3).reshape(B, S, H)
    return o, lse


def _dilated_gather_attn(q, kv_all, L, dil, band):
    T = q.shape[1]
    E = q.shape[-1]
    idx = L + jnp.arange(T)[:, None] - dil * jnp.arange(band + 1)[None, :]
    valid = idx >= 0
    kvg = jnp.take(kv_all, jnp.maximum(idx, 0), axis=1).astype(jnp.float32)
    s = jnp.einsum('bthe,btjhe->bthj', q.astype(jnp.float32), kvg[:, :, :, 0]) * (E ** -0.5)
    s = jnp.where(valid[None, :, None, :], s, -jnp.inf)
    m = jnp.max(s, axis=-1, keepdims=True)
    p = jnp.exp(s - m)
    den = jnp.sum(p, axis=-1, keepdims=True)
    o = jnp.einsum('bthj,btjhe->bthe', p, kvg[:, :, :, 1]) / den
    return o, (m + jnp.log(den))[..., 0]


def _combine_groups(outs, lses, w_out, dtype):
    o = jnp.stack(outs)
    wts = jax.nn.softmax(jnp.stack(lses), axis=0)
    mixed = jnp.sum(wts[..., None] * o, axis=0)
    B, T = mixed.shape[:2]
    return mixed.reshape(B, T, C_HEADS * C_HD).astype(dtype) @ w_out


def _mixer_dilated_prompt(h, pos, w_in, w_out):
    T = h.shape[1]
    q, k, v = _c_qkv(h, pos, w_in)
    outs, lses, bufs = [], [], []
    for gi, (win, dil) in enumerate(C_PATTERNS):
        o, l = _dilated_band_attn(q[:, :, gi], k[:, :, gi], v[:, :, gi], dil, win // dil)
        outs.append(o)
        lses.append(l)
        keep = min(win, T)
        bufs.append(jnp.stack([k[:, T - keep:, gi], v[:, T - keep:, gi]], axis=2))
    return _combine_groups(outs, lses, w_out, h.dtype), bufs


def _mixer_dilated_sample(h, pos, caches, w_in, w_out):
    T = h.shape[1]
    q, k, v = _c_qkv(h, pos, w_in)
    outs, lses, bufs = [], [], []
    for gi, (win, dil) in enumerate(C_PATTERNS):
        cache = caches[gi]
        L = cache.shape[1]
        kv_new = jnp.stack([k[:, :, gi], v[:, :, gi]], axis=2).astype(cache.dtype)
        kv_all = jnp.concatenate([cache, kv_new], axis=1)
        o, l = _dilated_gather_attn(q[:, :, gi], kv_all, L, dil, win // dil)
        outs.append(o)
        lses.append(l)
        keep = min(win, L + T)
        bufs.append(kv_all[:, L + T - keep:])
    return _combine_groups(outs, lses, w_out, h.dtype), bufs


def _peer(x, w_q, sub_keys, u_tab, v_tab):
    lead = x.shape[:-1]
    xt = x.reshape(-1, D_MODEL)
    n = xt.shape[0]
    q = (xt @ w_q).reshape(n, PEER_HEADS, 2, PEER_HALF)
    s = jnp.einsum('nhck,hcmk->nhcm', q, sub_keys).astype(jnp.float32)
    ts, ti = lax.top_k(s, PEER_TOPK)
    cand_s = (ts[:, :, 0, :, None] + ts[:, :, 1, None, :]).reshape(n, PEER_HEADS, PEER_TOPK * PEER_TOPK)
    cand_i = (ti[:, :, 0, :, None] * N_KEYS + ti[:, :, 1, None, :]).reshape(n, PEER_HEADS, PEER_TOPK * PEER_TOPK)
    best_s, best_p = lax.top_k(cand_s, PEER_TOPK)
    idx = jnp.take_along_axis(cand_i, best_p, axis=-1)
    gate = jax.nn.softmax(best_s, axis=-1)
    nblk = -(-n // PEER_BLOCK)
    pad = nblk * PEER_BLOCK - n
    xb = jnp.pad(xt, ((0, pad), (0, 0))).reshape(nblk, PEER_BLOCK, D_MODEL)
    ib = jnp.pad(idx, ((0, pad), (0, 0), (0, 0))).reshape(nblk, PEER_BLOCK, PEER_HEADS, PEER_TOPK)
    gb = jnp.pad(gate, ((0, pad), (0, 0), (0, 0))).reshape(nblk, PEER_BLOCK, PEER_HEADS, PEER_TOPK)

    def block(args):
        xc, ic, gc = args
        ue = u_tab[ic]
        ve = v_tab[ic]
        a = jax.nn.gelu(jnp.einsum('nd,nhkd->nhk', xc, ue).astype(jnp.float32), approximate=False) * gc
        return jnp.einsum('nhk,nhkd->nd', a.astype(ve.dtype), ve)

    out = lax.map(block, (xb, ib, gb)).reshape(nblk * PEER_BLOCK, D_MODEL)[:n]
    return out.reshape(lead + (D_MODEL,))


def _tail(x, p, norm_ffn, peer_wq, peer_keys, peer_u, peer_v, norm_ple, ple_gate, ple_proj):
    x = x + _peer(_rmsnorm(x, norm_ffn), peer_wq, peer_keys, peer_u, peer_v).astype(x.dtype)
    gate = jax.nn.sigmoid((_rmsnorm(x, norm_ple) @ ple_gate).astype(jnp.float32))
    emb = (p.astype(x.dtype) @ ple_proj).astype(jnp.float32)
    return x + (gate * emb).astype(x.dtype)


def setup_inputs(seed: int = 0) -> dict:
    key = jax.random.key(seed)
    ks = iter(jax.random.split(key, 48))

    def nrm(shape, scale):
        return jax.random.normal(next(ks), shape, jnp.float32) * scale

    def gain(shape):
        return 1.0 + nrm(shape, 0.05)

    dt = jnp.exp(jax.random.uniform(next(ks), (N_EVEN, GDN_HEADS), jnp.float32,
                                    minval=float(np.log(1e-3)), maxval=float(np.log(1e-1))))
    return {
        'x_prompt': nrm((BATCH, SEQ, D_MODEL), 1.0),
        'x_sample': nrm((DEC_BATCH, DEC_SEQ, D_MODEL), 1.0),
        'state_conv_a': nrm((N_EVEN, DEC_BATCH, CONV_A_WIDTH - 1, D_A), 1.0),
        'state_conv_qkv': nrm((N_EVEN, DEC_BATCH, GDN_CONV - 1, QKV_W), 1.0),
        'state_delta': nrm((N_EVEN, DEC_BATCH, GDN_HEADS, GDN_DK, GDN_DV), 0.05),
        'cache_kv_w128': nrm((N_ODD, DEC_BATCH, min(C_PATTERNS[0][0], PAST_LEN), 2, C_HEADS, C_HD), 1.0),
        'cache_kv_w512': nrm((N_ODD, DEC_BATCH, min(C_PATTERNS[1][0], PAST_LEN), 2, C_HEADS, C_HD), 1.0),
        'cache_kv_w2048': nrm((N_ODD, DEC_BATCH, min(C_PATTERNS[2][0], PAST_LEN), 2, C_HEADS, C_HD), 1.0),
        'p_prompt': nrm((DEPTH, BATCH, SEQ, PLE_DIM), 1.0),
        'p_sample': nrm((DEPTH, DEC_BATCH, DEC_SEQ, PLE_DIM), 1.0),
        'norm_mix': gain((DEPTH, D_MODEL)),
        'norm_ffn': gain((DEPTH, D_MODEL)),
        'norm_ple': gain((DEPTH, D_MODEL)),
        'norm_final': gain((D_MODEL,)),
        'w_in_ab': nrm((N_EVEN, D_MODEL, IN_AB), D_MODEL ** -0.5),
        'conv_a_w': nrm((N_EVEN, CONV_A_WIDTH, D_A), CONV_A_WIDTH ** -0.5),
        'conv_a_b': nrm((N_EVEN, D_A), 0.02),
        'ln_a_g': gain((N_EVEN, D_A)),
        'ln_a_b': nrm((N_EVEN, D_A), 0.02),
        'conv_qkv_w': nrm((N_EVEN, GDN_CONV, QKV_W), GDN_CONV ** -0.5),
        'a_log': jnp.log(jax.random.uniform(next(ks), (N_EVEN, GDN_HEADS), jnp.float32, minval=1.0, maxval=16.0)),
        'dt_bias': dt + jnp.log(-jnp.expm1(-dt)),
        'gdn_norm_g': gain((N_EVEN, GDN_DV)),
        'w_out_ab': nrm((N_EVEN, D_A + GDN_HEADS * GDN_DV, D_MODEL), 0.5 * (D_A + GDN_HEADS * GDN_DV) ** -0.5),
        'w_in_c': nrm((N_ODD, D_MODEL, IN_C), D_MODEL ** -0.5),
        'w_out_c': nrm((N_ODD, C_HEADS * C_HD, D_MODEL), 0.5 * (C_HEADS * C_HD) ** -0.5),
        'peer_wq': nrm((DEPTH, D_MODEL, PEER_HEADS * PEER_DKEY), D_MODEL ** -0.5),
        'peer_keys': nrm((DEPTH, PEER_HEADS, 2, N_KEYS, PEER_HALF), PEER_HALF ** -0.5),
        'peer_u': nrm((DEPTH, N_EXPERTS, D_MODEL), D_MODEL ** -0.5),
        'peer_v': nrm((DEPTH, N_EXPERTS, D_MODEL), 0.3),
        'ple_gate': nrm((DEPTH, D_MODEL, D_MODEL), D_MODEL ** -0.5),
        'ple_proj': nrm((DEPTH, PLE_DIM, D_MODEL), 0.5 * PLE_DIM ** -0.5),
    }


def reference(x_prompt, x_sample, state_conv_a, state_conv_qkv, state_delta, cache_kv_w128, cache_kv_w512,
              cache_kv_w2048, p_prompt, p_sample, norm_mix, norm_ffn, norm_ple, norm_final, w_in_ab, conv_a_w,
              conv_a_b, ln_a_g, ln_a_b, conv_qkv_w, a_log, dt_bias, gdn_norm_g, w_out_ab, w_in_c, w_out_c,
              peer_wq, peer_keys, peer_u, peer_v, ple_gate, ple_proj):
    Bp, S = x_prompt.shape[0], x_prompt.shape[1]
    T = x_sample.shape[1]
    pos_p = jnp.arange(S)
    pos_s = PAST_LEN + jnp.arange(T)
    kv_caches = (cache_kv_w128, cache_kv_w512, cache_kv_w2048)
    xp, xs = x_prompt, x_sample
    pa, pq, pd, sa, sq, sd = [], [], [], [], [], []
    pkv = [[] for _ in C_PATTERNS]
    skv = [[] for _ in C_PATTERNS]
    for i in range(DEPTH):
        hp = _rmsnorm(xp, norm_mix[i])
        hs = _rmsnorm(xs, norm_mix[i])
        if i % 2 == 0:
            e = i // 2
            wts = (w_in_ab[e], conv_a_w[e], conv_a_b[e], ln_a_g[e], ln_a_b[e], conv_qkv_w[e],
                   a_log[e], dt_bias[e], gdn_norm_g[e], w_out_ab[e])
            yp, ba, bq, st = _mixer_conv_delta(
                hp, jnp.zeros((Bp, CONV_A_WIDTH - 1, D_A), hp.dtype), jnp.zeros((Bp, GDN_CONV - 1, QKV_W), hp.dtype),
                jnp.zeros((Bp, GDN_HEADS, GDN_DK, GDN_DV), jnp.float32), *wts)
            pa.append(ba)
            pq.append(bq)
            pd.append(st)
            ys, ba, bq, st = _mixer_conv_delta(hs, state_conv_a[e], state_conv_qkv[e], state_delta[e], *wts)
            sa.append(ba)
            sq.append(bq)
            sd.append(st)
        else:
            o = i // 2
            yp, bufs = _mixer_dilated_prompt(hp, pos_p, w_in_c[o], w_out_c[o])
            ys, bufs_s = _mixer_dilated_sample(hs, pos_s, [c[o] for c in kv_caches], w_in_c[o], w_out_c[o])
            for gi in range(N_GROUPS):
                pkv[gi].append(bufs[gi])
                skv[gi].append(bufs_s[gi])
        xp = xp + yp.astype(xp.dtype)
        xs = xs + ys.astype(xs.dtype)
        xp = _tail(xp, p_prompt[i], norm_ffn[i], peer_wq[i], peer_keys[i], peer_u[i], peer_v[i],
                   norm_ple[i], ple_gate[i], ple_proj[i])
        xs = _tail(xs, p_sample[i], norm_ffn[i], peer_wq[i], peer_keys[i], peer_u[i], peer_v[i],
                   norm_ple[i], ple_gate[i], ple_proj[i])
    y_prompt = _rmsnorm(xp, norm_final)
    y_sample = _rmsnorm(xs, norm_final)
    return (y_prompt, y_sample,
            jnp.stack(pa), jnp.stack(pq), jnp.stack(pd),
            jnp.stack(pkv[0]), jnp.stack(pkv[1]), jnp.stack(pkv[2]),
            jnp.stack(sa), jnp.stack(sq), jnp.stack(sd),
            jnp.stack(skv[0]), jnp.stack(skv[1]), jnp.stack(skv[2]))
```

```python
import functools
import math

import jax
import jax.numpy as jnp
import numpy as np
from jax import lax
from jax.experimental import pallas as pl
from jax.experimental.pallas import tpu as pltpu

F32 = jnp.float32
BF16 = jnp.bfloat16

D_MODEL = 1024
EPS = 1e-6
D_A = 512
CONV_A_WIDTH = 31
GDN_HEADS = 4
GDN_DK = 128
GDN_DV = 128
GDN_CONV = 4
GDN_CHUNK = 64
QKV_W = GDN_HEADS * (2 * GDN_DK + GDN_DV)
C_PATTERNS = ((128, 1), (512, 4), (2048, 16))
C_HEADS = 8
C_HD = 128
ROT_DIM = 32
ROPE_THETA = 500000.0
ATT_BLOCK = 128
PAST_LEN = 8192
PEER_HEADS = 8
N_KEYS = 128
PEER_TOPK = 16
PEER_HALF = 64
PLE_DIM = 256

LANES = 128
SUBLANES = 8
VMEM_LIMIT_BYTES = 56 * 1024 * 1024

NEG = -1e30
LOG2E = 1.4426950408889634


def _cparams(*sem):
    return pltpu.CompilerParams(dimension_semantics=sem, vmem_limit_bytes=VMEM_LIMIT_BYTES)


def _rms(x, g):
    return x * lax.rsqrt(jnp.mean(x * x, axis=-1, keepdims=True) + EPS) * g


def _sigmoid(x):
    return 1.0 / (1.0 + jnp.exp(-x))


def _dot(a, b):
    return jnp.dot(a, b, preferred_element_type=F32)


def _dot_nt(a, b):
    return lax.dot_general(a, b, (((1,), (1,)), ((), ())), preferred_element_type=F32)


def _dot_tn(a, b):
    return lax.dot_general(a, b, (((0,), (0,)), ((), ())), preferred_element_type=F32)


def _norm_mm_body(x_ref, g_ref, *refs):
    nw = len(refs) // 2
    h = _rms(x_ref[...], g_ref[...]).astype(BF16)
    for w_ref, o_ref in zip(refs[:nw], refs[nw:]):
        o_ref[...] = _dot(h, w_ref[...])


def _norm_mm(x2d, gain, ws, tm):
    n = x2d.shape[0]
    in_specs = [pl.BlockSpec((tm, D_MODEL), lambda i: (i, 0)), pl.BlockSpec((1, D_MODEL), lambda i: (0, 0))]
    in_specs += [pl.BlockSpec(w.shape, lambda i: (0, 0)) for w in ws]
    out_specs = [pl.BlockSpec((tm, w.shape[1]), lambda i: (i, 0)) for w in ws]
    out_shape = [jax.ShapeDtypeStruct((n, w.shape[1]), F32) for w in ws]
    return pl.pallas_call(
        _norm_mm_body, grid=(n // tm,), in_specs=in_specs, out_specs=out_specs, out_shape=out_shape,
        compiler_params=_cparams("parallel"), name="norm_mm")(x2d, gain.reshape(1, D_MODEL), *ws)


def _norm_mm_rope_body(x_ref, g_ref, cos_ref, sa_ref, sb_ref, w_ref, o_ref, *, n_rope, n_heads):
    h = _rms(x_ref[...], g_ref[...]).astype(BF16)
    cos = cos_ref[...]
    sa = sa_ref[...]
    sb = sb_ref[...]
    for hh in range(n_heads):
        cs = slice(hh * C_HD, (hh + 1) * C_HD)
        o = _dot(h, w_ref[:, cs])
        if hh < n_rope:
            half = ROT_DIM // 2
            o = o * cos + pltpu.roll(o, half, 1) * sa + pltpu.roll(o, C_HD - half, 1) * sb
        o_ref[:, cs] = o


def _rope_tables(pos):
    half = ROT_DIM // 2
    inv = 1.0 / (ROPE_THETA ** (jnp.arange(0, ROT_DIM, 2, dtype=F32) / ROT_DIM))
    ang = pos.astype(F32)[:, None] * inv[None, :]
    c, s = jnp.cos(ang), jnp.sin(ang)
    p = pos.shape[0]
    ones = jnp.ones((p, C_HD - ROT_DIM), F32)
    zeros = jnp.zeros((p, C_HD - ROT_DIM), F32)
    zh = jnp.zeros((p, half), F32)
    cos = jnp.concatenate([c, c, ones], axis=1)
    sa = jnp.concatenate([zh, s, zeros], axis=1)
    sb = jnp.concatenate([-s, zh, zeros], axis=1)
    return cos, sa, sb


def _norm_mm_rope(x2d, gain, tables, w, n_rope, tm):
    n = x2d.shape[0]
    ncol = w.shape[1]
    cos, sa, sb = tables
    pblk = cos.shape[0] // tm
    tspec = pl.BlockSpec((tm, C_HD), lambda i: (i % pblk, 0))
    return pl.pallas_call(
        functools.partial(_norm_mm_rope_body, n_rope=n_rope, n_heads=ncol // C_HD),
        grid=(n // tm,),
        in_specs=[pl.BlockSpec((tm, D_MODEL), lambda i: (i, 0)), pl.BlockSpec((1, D_MODEL), lambda i: (0, 0)),
                  tspec, tspec, tspec, pl.BlockSpec(w.shape, lambda i: (0, 0))],
        out_specs=pl.BlockSpec((tm, ncol), lambda i: (i, 0)),
        out_shape=jax.ShapeDtypeStruct((n, ncol), F32),
        compiler_params=_cparams("parallel"), name="norm_mm_rope")(x2d, gain.reshape(1, D_MODEL), cos, sa, sb, w)


CONV_HIST = 32


def _conv_a_body(z_ref, hist_ref, w_ref, b_ref, lg_ref, lb_ref, ya_ref, glu_ref, ext_ref, *, ts):
    t = pl.program_id(1)

    @pl.when(t == 0)
    def _():
        ext_ref[0:CONV_HIST, :] = hist_ref[0]

    @pl.when(t > 0)
    def _():
        ext_ref[0:CONV_HIST, :] = ext_ref[ts:ts + CONV_HIST, :]

    z = z_ref[0]
    glu = z[:, :D_A] * _sigmoid(z[:, D_A:])
    glu_ref[0] = glu
    ext_ref[CONV_HIST:CONV_HIST + ts, :] = glu
    ch = min(ts, 64)
    first = CONV_HIST - (CONV_A_WIDTH - 1)
    for c in range(ts // ch):
        acc = jnp.zeros((ch, D_A), F32)
        for j in range(CONV_A_WIDTH):
            r0 = first + j + c * ch
            acc = acc + w_ref[j:j + 1, :] * ext_ref[r0:r0 + ch, :]
        y = acc + b_ref[...]
        yc = y - jnp.mean(y, axis=-1, keepdims=True)
        ln = yc * lax.rsqrt(jnp.mean(yc * yc, axis=-1, keepdims=True) + EPS) * lg_ref[...] + lb_ref[...]
        ya_ref[0, c * ch:(c + 1) * ch, :] = ln * _sigmoid(ln)


def _conv_a(zglu, hist, conv_w, conv_b, ln_g, ln_b, ts):
    b, t, _ = zglu.shape
    row = lambda a: a.reshape(1, D_A)
    wpad = jnp.pad(conv_w, ((0, CONV_HIST - CONV_A_WIDTH), (0, 0)))
    vec = pl.BlockSpec((1, D_A), lambda i, j: (0, 0))
    return pl.pallas_call(
        functools.partial(_conv_a_body, ts=ts),
        grid=(b, t // ts),
        in_specs=[pl.BlockSpec((1, ts, 2 * D_A), lambda i, j: (i, j, 0)),
                  pl.BlockSpec((1, CONV_HIST, D_A), lambda i, j: (i, 0, 0)),
                  pl.BlockSpec((CONV_HIST, D_A), lambda i, j: (0, 0)), vec, vec, vec],
        out_specs=[pl.BlockSpec((1, ts, D_A), lambda i, j: (i, j, 0)),
                   pl.BlockSpec((1, ts, D_A), lambda i, j: (i, j, 0))],
        out_shape=[jax.ShapeDtypeStruct((b, t, D_A), F32), jax.ShapeDtypeStruct((b, t, D_A), F32)],
        scratch_shapes=[pltpu.VMEM((CONV_HIST + ts, D_A), F32)],
        compiler_params=_cparams("arbitrary", "arbitrary"), name="conv_a")(
            zglu, hist, wpad, row(conv_b), row(ln_g), row(ln_b))


def _gdn_body(zq_ref, zg_ref, zb_ref, hist_ref, s0_ref, cw_ref, alog_ref, dtb_ref, ng_ref,
              od_ref, s_ref, ext_ref, *, t_valid):
    c = pl.program_id(1)
    ck = GDN_CHUNK

    @pl.when(c == 0)
    def _():
        ext_ref[0:SUBLANES, :] = hist_ref[0]
        s_ref[0] = s0_ref[0]

    @pl.when(c > 0)
    def _():
        ext_ref[0:SUBLANES, :] = ext_ref[ck:ck + SUBLANES, :]

    ext_ref[SUBLANES:SUBLANES + ck, :] = zq_ref[0]
    conv = jnp.zeros((ck, QKV_W), F32)
    for j in range(GDN_CONV):
        r0 = SUBLANES - (GDN_CONV - 1) + j
        conv = conv + cw_ref[j:j + 1, :] * ext_ref[r0:r0 + ck, :]
    qkv = conv * _sigmoid(conv)

    zb = zb_ref[0]
    row = c * ck + lax.broadcasted_iota(jnp.int32, (ck, LANES), 0)
    valid = row < t_valid
    beta_all = jnp.where(valid, _sigmoid(zb), 0.0)
    xg = zb + dtb_ref[...]
    softplus = jnp.maximum(xg, 0.0) + jnp.log(1.0 + jnp.exp(-jnp.abs(xg)))
    g_all = jnp.where(valid, -jnp.exp(alog_ref[...]) * softplus, 0.0)

    ri = lax.broadcasted_iota(jnp.int32, (ck, ck), 0)
    ci = lax.broadcasted_iota(jnp.int32, (ck, ck), 1)
    causal = ri >= ci
    strict = ri > ci
    tri = causal.astype(F32)
    eye = (ri == ci).astype(F32)
    ri2 = lax.broadcasted_iota(jnp.int32, (ck, 2 * ck), 0)
    ci2 = lax.broadcasted_iota(jnp.int32, (ck, 2 * ck), 1)
    upper2 = ((ri2 > ci2) | (ci2 == ck)).astype(F32)

    nq = GDN_HEADS * GDN_DK
    for h in range(GDN_HEADS):
        q = qkv[:, h * GDN_DK:(h + 1) * GDN_DK]
        k = qkv[:, nq + h * GDN_DK:nq + (h + 1) * GDN_DK]
        v = qkv[:, 2 * nq + h * GDN_DV:2 * nq + (h + 1) * GDN_DV]
        q = q * lax.rsqrt(jnp.sum(q * q, axis=-1, keepdims=True) + EPS) * (GDN_DK ** -0.5)
        k = k * lax.rsqrt(jnp.sum(k * k, axis=-1, keepdims=True) + EPS)
        beta = beta_all[:, h:h + 1]
        g = g_all[:, GDN_HEADS + h:GDN_HEADS + h + 1]
        dm = jnp.dot(tri, g * upper2, preferred_element_type=F32, precision=lax.Precision.HIGHEST)
        gcum = dm[:, ck:ck + 1]
        glast = gcum[ck - 1:ck, :]
        decay = jnp.where(causal, jnp.exp(dm[:, :ck]), 0.0)
        egc = jnp.exp(gcum)
        kb = k * beta
        lmat = jnp.where(strict, _dot_nt(kb, k) * decay, 0.0)
        p = -lmat
        tinv = eye + p
        for _ in range(int(math.log2(ck)) - 1):
            p = _dot(p, p)
            tinv = tinv + _dot(tinv, p)
        u = _dot(tinv, v * beta)
        w = _dot(tinv, kb * egc)
        intra = jnp.where(causal, _dot_nt(q, k) * decay, 0.0)
        s_old = s_ref[0, h]
        vnew = u - _dot(w, s_old)
        o = _dot(q * egc, s_old) + _dot(intra, vnew)
        s_ref[0, h] = s_old * jnp.exp(glast) + _dot_tn(k * jnp.exp(glast - gcum), vnew)
        gate = zg_ref[0, :, h * GDN_DV:(h + 1) * GDN_DV]
        on = o * lax.rsqrt(jnp.mean(o * o, axis=-1, keepdims=True) + EPS) * ng_ref[...]
        od_ref[0, :, h * GDN_DV:(h + 1) * GDN_DV] = on * (gate * _sigmoid(gate))


def _gdn(zqkv, zgate, zbg, hist, s0, conv_w, a_log, dt_bias, norm_g, t_valid):
    b, t, _ = zqkv.shape
    ck = GDN_CHUNK
    lane_vec = lambda a: jnp.zeros((1, LANES), F32).at[0, GDN_HEADS:2 * GDN_HEADS].set(a)
    return pl.pallas_call(
        functools.partial(_gdn_body, t_valid=t_valid),
        grid=(b, t // ck),
        in_specs=[pl.BlockSpec((1, ck, QKV_W), lambda i, j: (i, j, 0)),
                  pl.BlockSpec((1, ck, GDN_HEADS * GDN_DV), lambda i, j: (i, j, 0)),
                  pl.BlockSpec((1, ck, LANES), lambda i, j: (i, j, 0)),
                  pl.BlockSpec((1, SUBLANES, QKV_W), lambda i, j: (i, 0, 0)),
                  pl.BlockSpec((1, GDN_HEADS, GDN_DK, GDN_DV), lambda i, j: (i, 0, 0, 0)),
                  pl.BlockSpec((GDN_CONV, QKV_W), lambda i, j: (0, 0)),
                  pl.BlockSpec((1, LANES), lambda i, j: (0, 0)),
                  pl.BlockSpec((1, LANES), lambda i, j: (0, 0)),
                  pl.BlockSpec((1, GDN_DV), lambda i, j: (0, 0))],
        out_specs=[pl.BlockSpec((1, ck, GDN_HEADS * GDN_DV), lambda i, j: (i, j, 0)),
                   pl.BlockSpec((1, GDN_HEADS, GDN_DK, GDN_DV), lambda i, j: (i, 0, 0, 0))],
        out_shape=[jax.ShapeDtypeStruct((b, t, GDN_HEADS * GDN_DV), F32),
                   jax.ShapeDtypeStruct((b, GDN_HEADS, GDN_DK, GDN_DV), F32)],
        scratch_shapes=[pltpu.VMEM((SUBLANES + ck, QKV_W), F32)],
        compiler_params=_cparams("arbitrary", "arbitrary"), name="gdn")(
            zqkv, zgate, zbg, hist, s0, conv_w, lane_vec(a_log), lane_vec(dt_bias), norm_g.reshape(1, GDN_DV))


def _out_proj_ab_body(ya_ref, od_ref, x_ref, wa_ref, wb_ref, o_ref):
    y = _dot(ya_ref[...].astype(BF16), wa_ref[...]) + _dot(od_ref[...].astype(BF16), wb_ref[...])
    o_ref[...] = x_ref[...] + y


def _out_proj_ab(ya, od, x2d, wa, wb, tm):
    n = x2d.shape[0]
    half = pl.BlockSpec((tm, D_A), lambda i: (i, 0))
    full = pl.BlockSpec((tm, D_MODEL), lambda i: (i, 0))
    wspec = pl.BlockSpec((D_A, D_MODEL), lambda i: (0, 0))
    return pl.pallas_call(
        _out_proj_ab_body, grid=(n // tm,), in_specs=[half, half, full, wspec, wspec], out_specs=full,
        out_shape=jax.ShapeDtypeStruct((n, D_MODEL), F32), compiler_params=_cparams("parallel"),
        name="out_proj_ab")(ya, od, x2d, wa, wb)


def _band_attn_body(q_ref, kc_ref, kp_ref, o_ref, l_ref):
    blk = pl.program_id(2)
    nhd = C_HEADS * C_HD
    ri = lax.broadcasted_iota(jnp.int32, (ATT_BLOCK, ATT_BLOCK), 0)
    ci = lax.broadcasted_iota(jnp.int32, (ATT_BLOCK, ATT_BLOCK), 1)
    cur_ok = ci <= ri
    prev_ok = (ci >= ri) & (blk > 0)
    scale = C_HD ** -0.5
    for h in range(C_HEADS):
        cs = slice(h * C_HD, (h + 1) * C_HD)
        vs = slice(nhd + h * C_HD, nhd + (h + 1) * C_HD)
        q = q_ref[0, :, cs].astype(BF16)
        s_c = jnp.where(cur_ok, _dot_nt(q, kc_ref[0, :, cs].astype(BF16)) * scale, NEG)
        s_p = jnp.where(prev_ok, _dot_nt(q, kp_ref[0, :, cs].astype(BF16)) * scale, NEG)
        m = jnp.maximum(jnp.max(s_c, axis=-1, keepdims=True), jnp.max(s_p, axis=-1, keepdims=True))
        p_c = jnp.exp(s_c - m)
        p_p = jnp.exp(s_p - m)
        den = jnp.sum(p_c, axis=-1, keepdims=True) + jnp.sum(p_p, axis=-1, keepdims=True)
        o = _dot(p_c.astype(BF16), kc_ref[0, :, vs].astype(BF16)) + _dot(p_p.astype(BF16), kp_ref[0, :, vs].astype(BF16))
        o_ref[0, :, cs] = o / den
        l_ref[0, :, cs] = jnp.broadcast_to(m + jnp.log(den), (ATT_BLOCK, C_HD))


def _band_attn(q_all, kv, gi, dil, b, s):
    nhd = C_HEADS * C_HD
    n_groups = len(C_PATTERNS)
    n = s // dil
    nblk = n // ATT_BLOCK
    qv = q_all.reshape(b, n, dil * n_groups * nhd)
    kvv = kv.reshape(b, n, dil * 2 * nhd)
    ospec = pl.BlockSpec((1, ATT_BLOCK, nhd), lambda i, r, j: (i, j, r))
    o, l = pl.pallas_call(
        _band_attn_body, grid=(b, dil, nblk),
        in_specs=[pl.BlockSpec((1, ATT_BLOCK, nhd), lambda i, r, j: (i, j, r * n_groups + gi)),
                  pl.BlockSpec((1, ATT_BLOCK, 2 * nhd), lambda i, r, j: (i, j, r)),
                  pl.BlockSpec((1, ATT_BLOCK, 2 * nhd), lambda i, r, j: (i, jnp.maximum(j - 1, 0), r))],
        out_specs=[ospec, ospec],
        out_shape=[jax.ShapeDtypeStruct((b, n, dil * nhd), F32)] * 2,
        compiler_params=_cparams("parallel", "parallel", "arbitrary"), name="band_attn")(qv, kvv, kvv)
    return o.reshape(b * s, nhd), l.reshape(b * s, nhd)


def _head_selectors():
    nhd = C_HEADS * C_HD
    lane_head = np.arange(nhd) // C_HD
    sel = (lane_head[:, None] == np.arange(LANES)[None, :]).astype(np.float32)
    return jnp.asarray(sel, BF16), jnp.asarray(sel.T, BF16), jnp.asarray(sel.T, F32)


def _gather_attn_body(q_ref, c_ref, n_ref, sel_ref, selt_ref, seltf_ref, o_ref, l_ref, *, dil, t_new):
    nhd = C_HEADS * C_HD
    scale = C_HD ** -0.5
    rows = c_ref.shape[1]
    crow = lax.broadcasted_iota(jnp.int32, (rows, LANES), 0)
    nrow = lax.broadcasted_iota(jnp.int32, (SUBLANES, LANES), 0)
    r8 = lax.broadcasted_iota(jnp.int32, (SUBLANES, LANES), 0)
    for t in range(t_new):
        q = q_ref[0, t:t + 1, :]
        off = 0 if dil == 1 else t * 2 * nhd
        kc = c_ref[0, :, off:off + nhd]
        vc = c_ref[0, :, off + nhd:off + 2 * nhd]
        kn = n_ref[0, :, :nhd]
        vn = n_ref[0, :, nhd:]
        s_c = _dot((kc * q).astype(BF16), sel_ref[...]) * scale
        s_n = _dot((kn * q).astype(BF16), sel_ref[...]) * scale
        if dil == 1:
            s_c = jnp.where(crow >= t, s_c, NEG)
            s_n = jnp.where(nrow <= t, s_n, NEG)
        else:
            s_n = jnp.where(nrow == t, s_n, NEG)
        m = jnp.maximum(jnp.max(s_c, axis=0, keepdims=True), jnp.max(s_n, axis=0, keepdims=True))
        p_c = jnp.exp(s_c - m)
        p_n = jnp.exp(s_n - m)
        den = jnp.sum(p_c, axis=0, keepdims=True) + jnp.sum(p_n, axis=0, keepdims=True)
        pb_c = _dot(p_c.astype(BF16), selt_ref[...])
        pb_n = _dot(p_n.astype(BF16), selt_ref[...])
        o_un = jnp.sum(pb_c * vc, axis=0, keepdims=True) + jnp.sum(pb_n * vn, axis=0, keepdims=True)
        lse = m + jnp.log(den)
        stats = jnp.where(r8 == 0, jnp.broadcast_to(den, (SUBLANES, LANES)),
                          jnp.where(r8 == 1, jnp.broadcast_to(lse, (SUBLANES, LANES)), 0.0))
        sb = jnp.dot(stats, seltf_ref[...], preferred_element_type=F32, precision=lax.Precision.HIGHEST)
        o_ref[0, t:t + 1, :] = o_un / sb[0:1, :]
        l_ref[0, t:t + 1, :] = sb[1:2, :]


def _gather_attn(q_g, cache, kv_new8, dil, sels):
    nhd = C_HEADS * C_HD
    b, t_new, _ = q_g.shape
    L = cache.shape[1]
    rows = L // dil
    width = dil * 2 * nhd
    cv = cache.reshape(b, rows, width)
    cw = 2 * nhd if dil == 1 else t_new * 2 * nhd
    sel, selt, seltf = sels
    ospec = pl.BlockSpec((1, t_new, nhd), lambda i: (i, 0, 0))
    return pl.pallas_call(
        functools.partial(_gather_attn_body, dil=dil, t_new=t_new), grid=(b,),
        in_specs=[pl.BlockSpec((1, t_new, nhd), lambda i: (i, 0, 0)),
                  pl.BlockSpec((1, rows, cw), lambda i: (i, 0, 0)),
                  pl.BlockSpec((1, SUBLANES, 2 * nhd), lambda i: (i, 0, 0)),
                  pl.BlockSpec(sel.shape, lambda i: (0, 0)),
                  pl.BlockSpec(selt.shape, lambda i: (0, 0)),
                  pl.BlockSpec(seltf.shape, lambda i: (0, 0))],
        out_specs=[ospec, ospec],
        out_shape=[jax.ShapeDtypeStruct((b, t_new, nhd), F32)] * 2,
        compiler_params=_cparams("parallel"), name="gather_attn")(q_g, cv, kv_new8, sel, selt, seltf)


def _combine_proj_body(o0, o1, o2, l0, l1, l2, x_ref, w_ref, out_ref):
    a, b, c = l0[...], l1[...], l2[...]
    m = jnp.maximum(jnp.maximum(a, b), c)
    ea, eb, ec = jnp.exp(a - m), jnp.exp(b - m), jnp.exp(c - m)
    mixed = (ea * o0[...] + eb * o1[...] + ec * o2[...]) / (ea + eb + ec)
    out_ref[...] = x_ref[...] + _dot(mixed.astype(BF16), w_ref[...])


def _combine_proj(outs, lses, x2d, w, tm):
    n = x2d.shape[0]
    full = pl.BlockSpec((tm, D_MODEL), lambda i: (i, 0))
    return pl.pallas_call(
        _combine_proj_body, grid=(n // tm,), in_specs=[full] * 7 + [pl.BlockSpec(w.shape, lambda i: (0, 0))],
        out_specs=full, out_shape=jax.ShapeDtypeStruct((n, D_MODEL), F32),
        compiler_params=_cparams("parallel"), name="combine_proj")(*outs, *lses, x2d, w)


CAND_ROWS = PEER_TOPK + SUBLANES * (PEER_TOPK - 1)


def _top_rows(s, k):
    rank = lax.broadcasted_iota(jnp.int32, (k, s.shape[1]), 0)
    top = jnp.zeros((k, s.shape[1]), F32)
    for i in range(k):
        m = jnp.max(s, axis=0, keepdims=True)
        top = jnp.where(rank == i, m, top)
        s = jnp.where(s == m, NEG, s)
    return top


def _kth_largest(cand, k):
    m = None
    for _ in range(k):
        m = jnp.max(cand, axis=0, keepdims=True)
        cand = jnp.where(cand == m, NEG, cand)
    return m


def _pair_sums(a, b):
    parts = [a[0:1, :] + b]
    for i in range(1, PEER_TOPK):
        parts.append(a[i:i + 1, :] + b[0:SUBLANES, :])
    return jnp.concatenate(parts, axis=0)


def _route_body(x_ref, g_ref, wqt_ref, keys_ref, hb_ref, a_ref, b_ref, th_ref, qt_ref):
    hb = _rms(x_ref[...], g_ref[...]).astype(BF16)
    hb_ref[...] = hb
    qt_ref[...] = _dot_nt(wqt_ref[...], hb).astype(BF16)

    head_row = lax.broadcasted_iota(jnp.int32, th_ref.shape, 0)

    def head(h, thetas):
        r1 = pl.multiple_of(h * 2 * PEER_HALF, 2 * PEER_HALF)
        r2 = pl.multiple_of(h * 2 * PEER_HALF + PEER_HALF, PEER_HALF)
        s1 = _dot(keys_ref[2 * h], qt_ref[pl.ds(r1, PEER_HALF), :])
        s2 = _dot(keys_ref[2 * h + 1], qt_ref[pl.ds(r2, PEER_HALF), :])
        a = _top_rows(s1, PEER_TOPK)
        b = _top_rows(s2, PEER_TOPK)
        cand = _pair_sums(a, b)
        tau = _kth_largest(cand, PEER_TOPK)
        mx = a[0:1, :] + b[0:1, :]
        z = jnp.sum(jnp.where(cand >= tau, jnp.exp(cand - mx), 0.0), axis=0, keepdims=True)
        c = -mx - jnp.log(z)
        a2 = (s1 + c) * LOG2E
        b2 = s2 * LOG2E
        theta = _kth_largest(_pair_sums((a + c) * LOG2E, b * LOG2E), PEER_TOPK)
        a_ref[h] = a2
        b_ref[h] = b2
        return jnp.where(head_row == h, theta, thetas)

    th_ref[...] = lax.fori_loop(0, PEER_HEADS, head, jnp.zeros(th_ref.shape, F32))


def _route(x2d, gain, wqt, keys, tm):
    n = x2d.shape[0]
    return pl.pallas_call(
        _route_body, grid=(n // tm,),
        in_specs=[pl.BlockSpec((tm, D_MODEL), lambda i: (i, 0)), pl.BlockSpec((1, D_MODEL), lambda i: (0, 0)),
                  pl.BlockSpec(wqt.shape, lambda i: (0, 0)), pl.BlockSpec(keys.shape, lambda i: (0, 0, 0))],
        out_specs=[pl.BlockSpec((tm, D_MODEL), lambda i: (i, 0)),
                   pl.BlockSpec((PEER_HEADS, N_KEYS, tm), lambda i: (0, 0, i)),
                   pl.BlockSpec((PEER_HEADS, N_KEYS, tm), lambda i: (0, 0, i)),
                   pl.BlockSpec((PEER_HEADS, tm), lambda i: (0, i))],
        out_shape=[jax.ShapeDtypeStruct((n, D_MODEL), BF16),
                   jax.ShapeDtypeStruct((PEER_HEADS, N_KEYS, n), F32),
                   jax.ShapeDtypeStruct((PEER_HEADS, N_KEYS, n), F32),
                   jax.ShapeDtypeStruct((PEER_HEADS, n), F32)],
        scratch_shapes=[pltpu.VMEM((D_MODEL, tm), BF16)],
        compiler_params=_cparams("parallel"), name="peer_route")(x2d, gain.reshape(1, D_MODEL), wqt, keys)


def _gelu(x):
    return 0.5 * x * (1.0 + lax.erf(x * (2.0 ** -0.5)))


def _peer_body(hb_ref, u_ref, vt_ref, a_ref, b_ref, th_ref, x_ref, o_ref, sc_ref, act_ref, acc_ref, *, tm, te):
    j = pl.program_id(1)

    @pl.when(j == 0)
    def _():
        acc_ref[...] = jnp.zeros_like(acc_ref)

    sc_ref[...] = _dot_nt(u_ref[...], hb_ref[...])

    for e in range(te // N_KEYS):
        rows = slice(e * N_KEYS, (e + 1) * N_KEYS)

        def token_group(c, carry, e=e, rows=rows):
            cs = pl.ds(pl.multiple_of(c * LANES, LANES), LANES)
            g = jnp.zeros((N_KEYS, LANES), F32)
            for h in range(PEER_HEADS):
                t = a_ref[h, e:e + 1, cs] + b_ref[h, :, cs]
                g = g + jnp.where(t >= th_ref[h:h + 1, cs], jnp.exp2(t), 0.0)
            act_ref[rows, cs] = (_gelu(sc_ref[rows, cs]) * g).astype(BF16)
            return carry

        lax.fori_loop(0, tm // LANES, token_group, 0)

    acc_ref[...] += _dot(vt_ref[...], act_ref[...])

    @pl.when(j == pl.num_programs(1) - 1)
    def _():
        o_ref[...] = x_ref[...] + acc_ref[...].T


def _peer_dense(hb, a2, b2, th, x2d, u_bf, vt_bf, tm, te):
    n = x2d.shape[0]
    n_exp = u_bf.shape[0]
    return pl.pallas_call(
        functools.partial(_peer_body, tm=tm, te=te), grid=(n // tm, n_exp // te),
        in_specs=[pl.BlockSpec((tm, D_MODEL), lambda i, j: (i, 0)),
                  pl.BlockSpec((te, D_MODEL), lambda i, j: (j, 0)),
                  pl.BlockSpec((D_MODEL, te), lambda i, j: (0, j)),
                  pl.BlockSpec((PEER_HEADS, te // N_KEYS, tm), lambda i, j: (0, j, i)),
                  pl.BlockSpec((PEER_HEADS, N_KEYS, tm), lambda i, j: (0, 0, i)),
                  pl.BlockSpec((PEER_HEADS, tm), lambda i, j: (0, i)),
                  pl.BlockSpec((tm, D_MODEL), lambda i, j: (i, 0))],
        out_specs=pl.BlockSpec((tm, D_MODEL), lambda i, j: (i, 0)),
        out_shape=jax.ShapeDtypeStruct((n, D_MODEL), F32),
        scratch_shapes=[pltpu.VMEM((te, tm), F32), pltpu.VMEM((te, tm), BF16), pltpu.VMEM((D_MODEL, tm), F32)],
        compiler_params=_cparams("parallel", "arbitrary"), name="peer_dense")(hb, u_bf, vt_bf, a2, b2, th, x2d)


def _ple_body(x_ref, p_ref, g_ref, wg_ref, wp_ref, gf_ref, o_ref, *, final_norm):
    x = x_ref[...]
    gate = _sigmoid(_dot(_rms(x, g_ref[...]).astype(BF16), wg_ref[...]))
    emb = _dot(p_ref[...].astype(BF16), wp_ref[...])
    y = x + gate * emb
    if final_norm:
        y = _rms(y, gf_ref[...])
    o_ref[...] = y


def _ple(x2d, p2d, gain, wg, wp, gain_final, final_norm, tm):
    n = x2d.shape[0]
    full = pl.BlockSpec((tm, D_MODEL), lambda i: (i, 0))
    vec = pl.BlockSpec((1, D_MODEL), lambda i: (0, 0))
    return pl.pallas_call(
        functools.partial(_ple_body, final_norm=final_norm), grid=(n // tm,),
        in_specs=[full, pl.BlockSpec((tm, PLE_DIM), lambda i: (i, 0)), vec,
                  pl.BlockSpec(wg.shape, lambda i: (0, 0)), pl.BlockSpec(wp.shape, lambda i: (0, 0)), vec],
        out_specs=full, out_shape=jax.ShapeDtypeStruct((n, D_MODEL), F32),
        compiler_params=_cparams("parallel"), name="ple")(
            x2d, p2d, gain.reshape(1, D_MODEL), wg, wp, gain_final.reshape(1, D_MODEL))


PEER_TE = 1024


def _tail(x2d, p2d, norm_ffn, wqt, keys, u_bf, vt_bf, norm_ple, wg, wp, norm_final, final_norm, tm_route, tm):
    hb, a2, b2, th = _route(x2d, norm_ffn, wqt, keys, tm_route)
    x2 = _peer_dense(hb, a2, b2, th, x2d, u_bf, vt_bf, tm, PEER_TE)
    return _ple(x2, p2d, norm_ple, wg, wp, norm_final, final_norm, tm)


def kernel(x_prompt, x_sample, state_conv_a, state_conv_qkv, state_delta, cache_kv_w128, cache_kv_w512,
           cache_kv_w2048, p_prompt, p_sample, norm_mix, norm_ffn, norm_ple, norm_final, w_in_ab, conv_a_w,
           conv_a_b, ln_a_g, ln_a_b, conv_qkv_w, a_log, dt_bias, gdn_norm_g, w_out_ab, w_in_c, w_out_c,
           peer_wq, peer_keys, peer_u, peer_v, ple_gate, ple_proj):
    bp, s, _ = x_prompt.shape
    bs, t_new, _ = x_sample.shape
    n_p = bp * s
    n_s = bs * t_new
    nhd = C_HEADS * C_HD
    tm_p = 512
    tm_route_p = 256

    o1 = 2 * D_A
    o2 = o1 + QKV_W
    o3 = o2 + GDN_HEADS * GDN_DV
    w_ab = w_in_ab[0].astype(BF16)
    w_bg = jnp.pad(w_ab[:, o3:], ((0, 0), (0, LANES - 2 * GDN_HEADS)))
    w_ab_parts = [w_ab[:, :o1], w_ab[:, o1:o2], w_ab[:, o2:o3], w_bg]
    w_out_a = w_out_ab[0, :D_A].astype(BF16)
    w_out_b = w_out_ab[0, D_A:].astype(BF16)
    w_c = w_in_c[0].astype(BF16)
    n_g = len(C_PATTERNS)
    w_q_all = w_c[:, :n_g * nhd]
    w_kv = [jnp.concatenate([w_c[:, (n_g + g) * nhd:(n_g + g + 1) * nhd],
                             w_c[:, (2 * n_g + g) * nhd:(2 * n_g + g + 1) * nhd]], axis=1) for g in range(n_g)]
    w_oc = w_out_c[0].astype(BF16)
    wqt = [peer_wq[i].T.astype(BF16) for i in range(2)]
    keys = [peer_keys[i].reshape(2 * PEER_HEADS, N_KEYS, PEER_HALF).astype(BF16) for i in range(2)]
    u_bf = [peer_u[i].astype(BF16) for i in range(2)]
    vt_bf = [peer_v[i].astype(BF16).T for i in range(2)]
    wg = [ple_gate[i].astype(BF16) for i in range(2)]
    wp = [ple_proj[i].astype(BF16) for i in range(2)]

    t_pad = GDN_CHUNK
    xs_pad = jnp.pad(x_sample, ((0, 0), (0, t_pad - t_new), (0, 0)))
    hist_a_s = jnp.pad(state_conv_a[0], ((0, 0), (CONV_HIST - (CONV_A_WIDTH - 1), 0), (0, 0)))
    hist_q_s = jnp.pad(state_conv_qkv[0], ((0, 0), (SUBLANES - (GDN_CONV - 1), 0), (0, 0)))

    def layer0(x3d, hist_a, hist_q, s0, t_valid, ts):
        b, t, _ = x3d.shape
        x2d = x3d.reshape(b * t, D_MODEL)
        zglu, zqkv, zgate, zbg = _norm_mm(x2d, norm_mix[0], w_ab_parts, 512)
        ya, glu = _conv_a(zglu.reshape(b, t, 2 * D_A), hist_a, conv_a_w[0], conv_a_b[0], ln_a_g[0], ln_a_b[0], ts)
        zqkv3 = zqkv.reshape(b, t, QKV_W)
        od, s_fin = _gdn(zqkv3, zgate.reshape(b, t, -1), zbg.reshape(b, t, LANES), hist_q, s0, conv_qkv_w[0],
                         a_log[0], dt_bias[0], gdn_norm_g[0], t_valid)
        x1 = _out_proj_ab(ya.reshape(b * t, D_A), od.reshape(b * t, -1), x2d, w_out_a, w_out_b, 512)
        return x1.reshape(b, t, D_MODEL), glu, zqkv3, s_fin

    xp1, glu_p, zqkv_p, sfin_p = layer0(
        x_prompt, jnp.zeros((bp, CONV_HIST, D_A), F32), jnp.zeros((bp, SUBLANES, QKV_W), F32),
        jnp.zeros((bp, GDN_HEADS, GDN_DK, GDN_DV), F32), s, 256)
    xs1, glu_s, zqkv_s, sfin_s = layer0(xs_pad, hist_a_s, hist_q_s, state_delta[0], t_new, t_pad)
    xs1 = xs1[:, :t_new]

    na = CONV_A_WIDTH - 1
    nq = GDN_CONV - 1
    new_a_p = glu_p[:, s - na:][None]
    new_q_p = zqkv_p[:, s - nq:][None]
    new_a_s = jnp.concatenate([state_conv_a[0], glu_s[:, :t_new]], axis=1)[:, t_new:][None]
    new_q_s = jnp.concatenate([state_conv_qkv[0], zqkv_s[:, :t_new]], axis=1)[:, t_new:][None]

    xp2 = _tail(xp1.reshape(n_p, D_MODEL), p_prompt[0].reshape(n_p, PLE_DIM), norm_ffn[0], wqt[0], keys[0], u_bf[0],
                vt_bf[0], norm_ple[0], wg[0], wp[0], norm_final, False, tm_route_p, tm_p)
    xs2 = _tail(xs1.reshape(n_s, D_MODEL), p_sample[0].reshape(n_s, PLE_DIM), norm_ffn[0], wqt[0], keys[0], u_bf[0],
                vt_bf[0], norm_ple[0], wg[0], wp[0], norm_final, False, n_s, n_s)

    tab_p = _rope_tables(jnp.arange(s))
    tab_s = _rope_tables(PAST_LEN + (jnp.arange(n_s) % t_new))
    q_p = _norm_mm_rope(xp2, norm_mix[1], tab_p, w_q_all, n_g * C_HEADS, tm_p)
    q_s = _norm_mm_rope(xs2, norm_mix[1], tab_s, w_q_all, n_g * C_HEADS, n_s)
    caches = (cache_kv_w128, cache_kv_w512, cache_kv_w2048)
    sels = _head_selectors()
    outs_p, lses_p, outs_s, lses_s, kv_p, kv_s = [], [], [], [], [], []
    for g, (win, dil) in enumerate(C_PATTERNS):
        kvp = _norm_mm_rope(xp2, norm_mix[1], tab_p, w_kv[g], C_HEADS, tm_p)
        o, l = _band_attn(q_p, kvp, g, dil, bp, s)
        outs_p.append(o)
        lses_p.append(l)
        keep = min(win, s)
        kv_p.append(kvp.reshape(bp, s, 2, C_HEADS, C_HD)[:, s - keep:][None])

        kvs = _norm_mm_rope(xs2, norm_mix[1], tab_s, w_kv[g], C_HEADS, n_s).reshape(bs, t_new, 2 * nhd)
        kvs8 = jnp.pad(kvs, ((0, 0), (0, SUBLANES - t_new), (0, 0)))
        q_g = q_s.reshape(bs, t_new, n_g * nhd)[:, :, g * nhd:(g + 1) * nhd]
        o, l = _gather_attn(q_g, caches[g][0], kvs8, dil, sels)
        outs_s.append(o.reshape(n_s, nhd))
        lses_s.append(l.reshape(n_s, nhd))
        cache = caches[g][0]
        kv_s.append(jnp.concatenate([cache[:, t_new:], kvs.reshape(bs, t_new, 2, C_HEADS, C_HD)], axis=1)[None])

    xp3 = _combine_proj(outs_p, lses_p, xp2, w_oc, tm_p)
    xs3 = _combine_proj(outs_s, lses_s, xs2, w_oc, n_s)
    y_p = _tail(xp3, p_prompt[1].reshape(n_p, PLE_DIM), norm_ffn[1], wqt[1], keys[1], u_bf[1], vt_bf[1],
                norm_ple[1], wg[1], wp[1], norm_final, True, tm_route_p, tm_p)
    y_s = _tail(xs3, p_sample[1].reshape(n_s, PLE_DIM), norm_ffn[1], wqt[1], keys[1], u_bf[1], vt_bf[1],
                norm_ple[1], wg[1], wp[1], norm_final, True, n_s, n_s)

    return (y_p.reshape(bp, s, D_MODEL), y_s.reshape(bs, t_new, D_MODEL),
            new_a_p, new_q_p, sfin_p[None], kv_p[0], kv_p[1], kv_p[2],
            new_a_s, new_q_s, sfin_s[None], kv_s[0], kv_s[1], kv_s[2])
```

```python
import functools
import math

import jax
import jax.numpy as jnp
import numpy as np
from jax import lax
from jax.experimental import pallas as pl
from jax.experimental.pallas import tpu as pltpu

F32 = jnp.float32
BF16 = jnp.bfloat16

D_MODEL = 1024
EPS = 1e-6
D_A = 512
CONV_A_WIDTH = 31
GDN_HEADS = 4
GDN_DK = 128
GDN_DV = 128
GDN_CONV = 4
GDN_CHUNK = 64
QKV_W = GDN_HEADS * (2 * GDN_DK + GDN_DV)
C_PATTERNS = ((128, 1), (512, 4), (2048, 16))
C_HEADS = 8
C_HD = 128
ROT_DIM = 32
ROPE_THETA = 500000.0
ATT_BLOCK = 128
PAST_LEN = 8192
PEER_HEADS = 8
N_KEYS = 128
PEER_TOPK = 16
PEER_HALF = 64
PLE_DIM = 256

LANES = 128
SUBLANES = 8
VMEM_LIMIT_BYTES = 56 * 1024 * 1024

NEG = -1e30
LOG2E = 1.4426950408889634


def _cparams(*sem):
    return pltpu.CompilerParams(dimension_semantics=sem, vmem_limit_bytes=VMEM_LIMIT_BYTES)


def _rms(x, g):
    return x * lax.rsqrt(jnp.mean(x * x, axis=-1, keepdims=True) + EPS) * g


def _sigmoid(x):
    return 1.0 / (1.0 + jnp.exp(-x))


def _dot(a, b):
    return jnp.dot(a, b, preferred_element_type=F32)


def _dot_nt(a, b):
    return lax.dot_general(a, b, (((1,), (1,)), ((), ())), preferred_element_type=F32)


def _dot_tn(a, b):
    return lax.dot_general(a, b, (((0,), (0,)), ((), ())), preferred_element_type=F32)


def _norm_mm_body(x_ref, g_ref, *refs):
    nw = len(refs) // 2
    h = _rms(x_ref[...], g_ref[...]).astype(BF16)
    for w_ref, o_ref in zip(refs[:nw], refs[nw:]):
        o_ref[...] = _dot(h, w_ref[...])


def _norm_mm(x2d, gain, ws, tm):
    n = x2d.shape[0]
    in_specs = [pl.BlockSpec((tm, D_MODEL), lambda i: (i, 0)), pl.BlockSpec((1, D_MODEL), lambda i: (0, 0))]
    in_specs += [pl.BlockSpec(w.shape, lambda i: (0, 0)) for w in ws]
    out_specs = [pl.BlockSpec((tm, w.shape[1]), lambda i: (i, 0)) for w in ws]
    out_shape = [jax.ShapeDtypeStruct((n, w.shape[1]), F32) for w in ws]
    return pl.pallas_call(
        _norm_mm_body, grid=(n // tm,), in_specs=in_specs, out_specs=out_specs, out_shape=out_shape,
        compiler_params=_cparams("parallel"), name="norm_mm")(x2d, gain.reshape(1, D_MODEL), *ws)


def _norm_mm_rope_body(x_ref, g_ref, cos_ref, sa_ref, sb_ref, w_ref, o_ref, *, n_rope, n_heads):
    h = _rms(x_ref[...], g_ref[...]).astype(BF16)
    cos = cos_ref[...]
    sa = sa_ref[...]
    sb = sb_ref[...]
    for hh in range(n_heads):
        cs = slice(hh * C_HD, (hh + 1) * C_HD)
        o = _dot(h, w_ref[:, cs])
        if hh < n_rope:
            half = ROT_DIM // 2
            o = o * cos + pltpu.roll(o, half, 1) * sa + pltpu.roll(o, C_HD - half, 1) * sb
        o_ref[:, cs] = o


def _rope_tables(pos):
    half = ROT_DIM // 2
    inv = 1.0 / (ROPE_THETA ** (jnp.arange(0, ROT_DIM, 2, dtype=F32) / ROT_DIM))
    ang = pos.astype(F32)[:, None] * inv[None, :]
    c, s = jnp.cos(ang), jnp.sin(ang)
    p = pos.shape[0]
    ones = jnp.ones((p, C_HD - ROT_DIM), F32)
    zeros = jnp.zeros((p, C_HD - ROT_DIM), F32)
    zh = jnp.zeros((p, half), F32)
    cos = jnp.concatenate([c, c, ones], axis=1)
    sa = jnp.concatenate([zh, s, zeros], axis=1)
    sb = jnp.concatenate([-s, zh, zeros], axis=1)
    return cos, sa, sb


def _norm_mm_rope(x2d, gain, tables, w, n_rope, tm):
    n = x2d.shape[0]
    ncol = w.shape[1]
    cos, sa, sb = tables
    pblk = cos.shape[0] // tm
    tspec = pl.BlockSpec((tm, C_HD), lambda i: (i % pblk, 0))
    return pl.pallas_call(
        functools.partial(_norm_mm_rope_body, n_rope=n_rope, n_heads=ncol // C_HD),
        grid=(n // tm,),
        in_specs=[pl.BlockSpec((tm, D_MODEL), lambda i: (i, 0)), pl.BlockSpec((1, D_MODEL), lambda i: (0, 0)),
                  tspec, tspec, tspec, pl.BlockSpec(w.shape, lambda i: (0, 0))],
        out_specs=pl.BlockSpec((tm, ncol), lambda i: (i, 0)),
        out_shape=jax.ShapeDtypeStruct((n, ncol), F32),
        compiler_params=_cparams("parallel"), name="norm_mm_rope")(x2d, gain.reshape(1, D_MODEL), cos, sa, sb, w)


CONV_HIST = 32


def _conv_a_body(z_ref, hist_ref, w_ref, b_ref, lg_ref, lb_ref, ya_ref, glu_ref, ext_ref, *, ts):
    t = pl.program_id(1)

    @pl.when(t == 0)
    def _():
        ext_ref[0:CONV_HIST, :] = hist_ref[0]

    @pl.when(t > 0)
    def _():
        ext_ref[0:CONV_HIST, :] = ext_ref[ts:ts + CONV_HIST, :]

    z = z_ref[0]
    glu = z[:, :D_A] * _sigmoid(z[:, D_A:])
    glu_ref[0] = glu
    ext_ref[CONV_HIST:CONV_HIST + ts, :] = glu
    ch = min(ts, 64)
    first = CONV_HIST - (CONV_A_WIDTH - 1)
    for c in range(ts // ch):
        acc = jnp.zeros((ch, D_A), F32)
        for j in range(CONV_A_WIDTH):
            r0 = first + j + c * ch
            acc = acc + w_ref[j:j + 1, :] * ext_ref[r0:r0 + ch, :]
        y = acc + b_ref[...]
        yc = y - jnp.mean(y, axis=-1, keepdims=True)
        ln = yc * lax.rsqrt(jnp.mean(yc * yc, axis=-1, keepdims=True) + EPS) * lg_ref[...] + lb_ref[...]
        ya_ref[0, c * ch:(c + 1) * ch, :] = ln * _sigmoid(ln)


def _conv_a(zglu, hist, conv_w, conv_b, ln_g, ln_b, ts):
    b, t, _ = zglu.shape
    row = lambda a: a.reshape(1, D_A)
    wpad = jnp.pad(conv_w, ((0, CONV_HIST - CONV_A_WIDTH), (0, 0)))
    vec = pl.BlockSpec((1, D_A), lambda i, j: (0, 0))
    return pl.pallas_call(
        functools.partial(_conv_a_body, ts=ts),
        grid=(b, t // ts),
        in_specs=[pl.BlockSpec((1, ts, 2 * D_A), lambda i, j: (i, j, 0)),
                  pl.BlockSpec((1, CONV_HIST, D_A), lambda i, j: (i, 0, 0)),
                  pl.BlockSpec((CONV_HIST, D_A), lambda i, j: (0, 0)), vec, vec, vec],
        out_specs=[pl.BlockSpec((1, ts, D_A), lambda i, j: (i, j, 0)),
                   pl.BlockSpec((1, ts, D_A), lambda i, j: (i, j, 0))],
        out_shape=[jax.ShapeDtypeStruct((b, t, D_A), F32), jax.ShapeDtypeStruct((b, t, D_A), F32)],
        scratch_shapes=[pltpu.VMEM((CONV_HIST + ts, D_A), F32)],
        compiler_params=_cparams("arbitrary", "arbitrary"), name="conv_a")(
            zglu, hist, wpad, row(conv_b), row(ln_g), row(ln_b))


def _gdn_body(zq_ref, zg_ref, zb_ref, hist_ref, s0_ref, cw_ref, alog_ref, dtb_ref, ng_ref,
              od_ref, s_ref, ext_ref, *, t_valid):
    c = pl.program_id(1)
    ck = GDN_CHUNK

    @pl.when(c == 0)
    def _():
        ext_ref[0:SUBLANES, :] = hist_ref[0]
        s_ref[0] = s0_ref[0]

    @pl.when(c > 0)
    def _():
        ext_ref[0:SUBLANES, :] = ext_ref[ck:ck + SUBLANES, :]

    ext_ref[SUBLANES:SUBLANES + ck, :] = zq_ref[0]
    conv = jnp.zeros((ck, QKV_W), F32)
    for j in range(GDN_CONV):
        r0 = SUBLANES - (GDN_CONV - 1) + j
        conv = conv + cw_ref[j:j + 1, :] * ext_ref[r0:r0 + ck, :]
    qkv = conv * _sigmoid(conv)

    zb = zb_ref[0]
    row = c * ck + lax.broadcasted_iota(jnp.int32, (ck, LANES), 0)
    valid = row < t_valid
    beta_all = jnp.where(valid, _sigmoid(zb), 0.0)
    xg = zb + dtb_ref[...]
    softplus = jnp.maximum(xg, 0.0) + jnp.log(1.0 + jnp.exp(-jnp.abs(xg)))
    g_all = jnp.where(valid, -jnp.exp(alog_ref[...]) * softplus, 0.0)

    ri = lax.broadcasted_iota(jnp.int32, (ck, ck), 0)
    ci = lax.broadcasted_iota(jnp.int32, (ck, ck), 1)
    causal = ri >= ci
    strict = ri > ci
    tri = causal.astype(F32)
    eye = (ri == ci).astype(F32)
    ri2 = lax.broadcasted_iota(jnp.int32, (ck, 2 * ck), 0)
    ci2 = lax.broadcasted_iota(jnp.int32, (ck, 2 * ck), 1)
    upper2 = ((ri2 > ci2) | (ci2 == ck)).astype(F32)

    nq = GDN_HEADS * GDN_DK
    for h in range(GDN_HEADS):
        q = qkv[:, h * GDN_DK:(h + 1) * GDN_DK]
        k = qkv[:, nq + h * GDN_DK:nq + (h + 1) * GDN_DK]
        v = qkv[:, 2 * nq + h * GDN_DV:2 * nq + (h + 1) * GDN_DV]
        q = q * lax.rsqrt(jnp.sum(q * q, axis=-1, keepdims=True) + EPS) * (GDN_DK ** -0.5)
        k = k * lax.rsqrt(jnp.sum(k * k, axis=-1, keepdims=True) + EPS)
        beta = beta_all[:, h:h + 1]
        g = g_all[:, GDN_HEADS + h:GDN_HEADS + h + 1]
        dm = jnp.dot(tri, g * upper2, preferred_element_type=F32, precision=lax.Precision.HIGHEST)
        gcum = dm[:, ck:ck + 1]
        glast = gcum[ck - 1:ck, :]
        decay = jnp.where(causal, jnp.exp(dm[:, :ck]), 0.0)
        egc = jnp.exp(gcum)
        kb = k * beta
        lmat = jnp.where(strict, _dot_nt(kb, k) * decay, 0.0)
        p = -lmat
        tinv = eye + p
        for _ in range(int(math.log2(ck)) - 1):
            p = _dot(p, p)
            tinv = tinv + _dot(tinv, p)
        u = _dot(tinv, v * beta)
        w = _dot(tinv, kb * egc)
        intra = jnp.where(causal, _dot_nt(q, k) * decay, 0.0)
        s_old = s_ref[0, h]
        vnew = u - _dot(w, s_old)
        o = _dot(q * egc, s_old) + _dot(intra, vnew)
        s_ref[0, h] = s_old * jnp.exp(glast) + _dot_tn(k * jnp.exp(glast - gcum), vnew)
        gate = zg_ref[0, :, h * GDN_DV:(h + 1) * GDN_DV]
        on = o * lax.rsqrt(jnp.mean(o * o, axis=-1, keepdims=True) + EPS) * ng_ref[...]
        od_ref[0, :, h * GDN_DV:(h + 1) * GDN_DV] = on * (gate * _sigmoid(gate))


def _gdn(zqkv, zgate, zbg, hist, s0, conv_w, a_log, dt_bias, norm_g, t_valid):
    b, t, _ = zqkv.shape
    ck = GDN_CHUNK
    lane_vec = lambda a: jnp.zeros((1, LANES), F32).at[0, GDN_HEADS:2 * GDN_HEADS].set(a)
    return pl.pallas_call(
        functools.partial(_gdn_body, t_valid=t_valid),
        grid=(b, t // ck),
        in_specs=[pl.BlockSpec((1, ck, QKV_W), lambda i, j: (i, j, 0)),
                  pl.BlockSpec((1, ck, GDN_HEADS * GDN_DV), lambda i, j: (i, j, 0)),
                  pl.BlockSpec((1, ck, LANES), lambda i, j: (i, j, 0)),
                  pl.BlockSpec((1, SUBLANES, QKV_W), lambda i, j: (i, 0, 0)),
                  pl.BlockSpec((1, GDN_HEADS, GDN_DK, GDN_DV), lambda i, j: (i, 0, 0, 0)),
                  pl.BlockSpec((GDN_CONV, QKV_W), lambda i, j: (0, 0)),
                  pl.BlockSpec((1, LANES), lambda i, j: (0, 0)),
                  pl.BlockSpec((1, LANES), lambda i, j: (0, 0)),
                  pl.BlockSpec((1, GDN_DV), lambda i, j: (0, 0))],
        out_specs=[pl.BlockSpec((1, ck, GDN_HEADS * GDN_DV), lambda i, j: (i, j, 0)),
                   pl.BlockSpec((1, GDN_HEADS, GDN_DK, GDN_DV), lambda i, j: (i, 0, 0, 0))],
        out_shape=[jax.ShapeDtypeStruct((b, t, GDN_HEADS * GDN_DV), F32),
                   jax.ShapeDtypeStruct((b, GDN_HEADS, GDN_DK, GDN_DV), F32)],
        scratch_shapes=[pltpu.VMEM((SUBLANES + ck, QKV_W), F32)],
        compiler_params=_cparams("arbitrary", "arbitrary"), name="gdn")(
            zqkv, zgate, zbg, hist, s0, conv_w, lane_vec(a_log), lane_vec(dt_bias), norm_g.reshape(1, GDN_DV))


def _out_proj_ab_body(ya_ref, od_ref, x_ref, wa_ref, wb_ref, o_ref):
    y = _dot(ya_ref[...].astype(BF16), wa_ref[...]) + _dot(od_ref[...].astype(BF16), wb_ref[...])
    o_ref[...] = x_ref[...] + y


def _out_proj_ab(ya, od, x2d, wa, wb, tm):
    n = x2d.shape[0]
    half = pl.BlockSpec((tm, D_A), lambda i: (i, 0))
    full = pl.BlockSpec((tm, D_MODEL), lambda i: (i, 0))
    wspec = pl.BlockSpec((D_A, D_MODEL), lambda i: (0, 0))
    return pl.pallas_call(
        _out_proj_ab_body, grid=(n // tm,), in_specs=[half, half, full, wspec, wspec], out_specs=full,
        out_shape=jax.ShapeDtypeStruct((n, D_MODEL), F32), compiler_params=_cparams("parallel"),
        name="out_proj_ab")(ya, od, x2d, wa, wb)


def _band_attn_body(q_ref, kc_ref, kp_ref, o_ref, l_ref):
    blk = pl.program_id(1)
    nhd = C_HEADS * C_HD
    ri = lax.broadcasted_iota(jnp.int32, (ATT_BLOCK, ATT_BLOCK), 0)
    ci = lax.broadcasted_iota(jnp.int32, (ATT_BLOCK, ATT_BLOCK), 1)
    cur_ok = ci <= ri
    prev_ok = (ci >= ri) & (blk > 0)
    scale = C_HD ** -0.5
    for h in range(C_HEADS):
        cs = slice(h * C_HD, (h + 1) * C_HD)
        vs = slice(nhd + h * C_HD, nhd + (h + 1) * C_HD)
        q = q_ref[0, :, cs].astype(BF16)
        s_c = jnp.where(cur_ok, _dot_nt(q, kc_ref[0, :, cs].astype(BF16)) * scale, NEG)
        s_p = jnp.where(prev_ok, _dot_nt(q, kp_ref[0, :, cs].astype(BF16)) * scale, NEG)
        m = jnp.maximum(jnp.max(s_c, axis=-1, keepdims=True), jnp.max(s_p, axis=-1, keepdims=True))
        p_c = jnp.exp(s_c - m)
        p_p = jnp.exp(s_p - m)
        den = jnp.sum(p_c, axis=-1, keepdims=True) + jnp.sum(p_p, axis=-1, keepdims=True)
        o = _dot(p_c.astype(BF16), kc_ref[0, :, vs].astype(BF16)) + _dot(p_p.astype(BF16), kp_ref[0, :, vs].astype(BF16))
        o_ref[0, :, cs] = o / den
        l_ref[0, :, cs] = jnp.broadcast_to(m + jnp.log(den), (ATT_BLOCK, C_HD))


def _band_attn_dilated_body(q_ref, k_ref, v_ref, kp_ref, vp_ref, o_ref, l_ref, *, dil, has_prev):
    blk = pl.program_id(1)
    ri = lax.broadcasted_iota(jnp.int32, (ATT_BLOCK, ATT_BLOCK), 0)
    ci = lax.broadcasted_iota(jnp.int32, (ATT_BLOCK, ATT_BLOCK), 1)
    cur_ok = ci <= ri
    prev_ok = (ci >= ri) & (blk > 0)
    scale = C_HD ** -0.5

    def stream(r, carry):
        rows = pl.ds(r, ATT_BLOCK, stride=dil)
        q = q_ref[0, rows, :].astype(BF16)
        s_c = jnp.where(cur_ok, _dot_nt(q, k_ref[0, rows, :].astype(BF16)) * scale, NEG)
        m = jnp.max(s_c, axis=-1, keepdims=True)
        if has_prev:
            s_p = jnp.where(prev_ok, _dot_nt(q, kp_ref[0, rows, :].astype(BF16)) * scale, NEG)
            m = jnp.maximum(m, jnp.max(s_p, axis=-1, keepdims=True))
        p_c = jnp.exp(s_c - m)
        den = jnp.sum(p_c, axis=-1, keepdims=True)
        o = _dot(p_c.astype(BF16), v_ref[0, rows, :].astype(BF16))
        if has_prev:
            p_p = jnp.exp(s_p - m)
            den = den + jnp.sum(p_p, axis=-1, keepdims=True)
            o = o + _dot(p_p.astype(BF16), vp_ref[0, rows, :].astype(BF16))
        o_ref[0, rows, :] = o / den
        l_ref[0, rows, :] = jnp.broadcast_to(m + jnp.log(den), (ATT_BLOCK, C_HD))
        return carry

    lax.fori_loop(0, dil, stream, 0)


def _band_attn(q_all, kv, gi, dil, b, s):
    nhd = C_HEADS * C_HD
    n_groups = len(C_PATTERNS)
    q3 = q_all.reshape(b, s, n_groups * nhd)
    kv3 = kv.reshape(b, s, 2 * nhd)
    if dil == 1:
        nblk = s // ATT_BLOCK
        ospec = pl.BlockSpec((1, ATT_BLOCK, nhd), lambda i, j: (i, j, 0))
        o, l = pl.pallas_call(
            _band_attn_body, grid=(b, nblk),
            in_specs=[pl.BlockSpec((1, ATT_BLOCK, nhd), lambda i, j: (i, j, gi)),
                      pl.BlockSpec((1, ATT_BLOCK, 2 * nhd), lambda i, j: (i, j, 0)),
                      pl.BlockSpec((1, ATT_BLOCK, 2 * nhd), lambda i, j: (i, jnp.maximum(j - 1, 0), 0))],
            out_specs=[ospec, ospec],
            out_shape=[jax.ShapeDtypeStruct((b, s, nhd), F32)] * 2,
            compiler_params=_cparams("parallel", "arbitrary"), name="band_attn")(q3, kv3, kv3)
    else:
        rows = dil * ATT_BLOCK
        nblk = s // rows
        cur = lambda c0: pl.BlockSpec((1, rows, C_HD), lambda i, j, h: (i, j, c0 + h))
        prev = lambda c0: pl.BlockSpec((1, rows, C_HD), lambda i, j, h: (i, jnp.maximum(j - 1, 0), c0 + h))
        o, l = pl.pallas_call(
            functools.partial(_band_attn_dilated_body, dil=dil, has_prev=nblk > 1), grid=(b, nblk, C_HEADS),
            in_specs=[cur(gi * C_HEADS), cur(0), cur(C_HEADS), prev(0), prev(C_HEADS)],
            out_specs=[cur(0), cur(0)],
            out_shape=[jax.ShapeDtypeStruct((b, s, nhd), F32)] * 2,
            compiler_params=_cparams("parallel", "arbitrary", "arbitrary"), name="band_attn_dilated")(
                q3, kv3, kv3, kv3, kv3)
    return o.reshape(b * s, nhd), l.reshape(b * s, nhd)


def _gather_attn_body(q_ref, c_ref, n_ref, ones_ref, o_ref, l_ref, *, dil, t_new):
    scale = C_HD ** -0.5
    rows = c_ref.shape[1]
    crow = lax.broadcasted_iota(jnp.int32, (rows, C_HEADS, C_HD), 0)
    nrow = lax.broadcasted_iota(jnp.int32, (t_new, C_HEADS, C_HD), 0)

    def head_dots(k, q):
        prod = (k * q[None]).astype(BF16).reshape(k.shape[0] * C_HEADS, C_HD)
        return (_dot(prod, ones_ref[...]) * scale).reshape(k.shape)

    for t in range(t_new):
        r = 0 if dil == 1 else t
        q = q_ref[0, t]
        s_c = head_dots(c_ref[0, :, r, 0], q)
        s_n = head_dots(n_ref[0, :, 0], q)
        if dil == 1:
            s_c = jnp.where(crow >= t, s_c, NEG)
            s_n = jnp.where(nrow <= t, s_n, NEG)
        else:
            s_n = jnp.where(nrow == t, s_n, NEG)
        m = jnp.maximum(jnp.max(s_c, axis=0), jnp.max(s_n, axis=0))
        p_c = jnp.exp(s_c - m[None])
        p_n = jnp.exp(s_n - m[None])
        den = jnp.sum(p_c, axis=0) + jnp.sum(p_n, axis=0)
        o = jnp.sum(p_c * c_ref[0, :, r, 1], axis=0) + jnp.sum(p_n * n_ref[0, :, 1], axis=0)
        o_ref[0, t] = o / den
        l_ref[0, t] = m + jnp.log(den)


def _gather_attn(q_g, cache, kv_new, dil):
    b, t_new = q_g.shape[:2]
    L = cache.shape[1]
    rows = L // dil
    streams = 1 if dil == 1 else t_new
    cv = cache.reshape(b, rows, dil, 2, C_HEADS, C_HD)
    ones = jnp.ones((C_HD, C_HD), BF16)
    ospec = pl.BlockSpec((1, t_new, C_HEADS, C_HD), lambda i: (i, 0, 0, 0))
    return pl.pallas_call(
        functools.partial(_gather_attn_body, dil=dil, t_new=t_new), grid=(b,),
        in_specs=[ospec,
                  pl.BlockSpec((1, rows, streams, 2, C_HEADS, C_HD), lambda i: (i, 0, 0, 0, 0, 0)),
                  pl.BlockSpec((1, t_new, 2, C_HEADS, C_HD), lambda i: (i, 0, 0, 0, 0)),
                  pl.BlockSpec(ones.shape, lambda i: (0, 0))],
        out_specs=[ospec, ospec],
        out_shape=[jax.ShapeDtypeStruct((b, t_new, C_HEADS, C_HD), F32)] * 2,
        compiler_params=_cparams("parallel"), name="gather_attn")(q_g, cv, kv_new, ones)


ROLL_SPLIT = 4


def _roll_body(*refs, n_cache, t_new):
    caches, news, outs, sem = refs[:n_cache], refs[n_cache:2 * n_cache], refs[2 * n_cache:3 * n_cache], refs[-1]
    copies = []
    for ci in range(n_cache):
        b, L = caches[ci].shape[:2]
        step = b // ROLL_SPLIT
        for part in range(ROLL_SPLIT):
            bs = pl.ds(part * step, step)
            copies.append(pltpu.make_async_copy(caches[ci].at[bs, pl.ds(t_new, L - t_new)],
                                                outs[ci].at[bs, pl.ds(0, L - t_new)], sem.at[len(copies)]))
        copies.append(pltpu.make_async_copy(news[ci], outs[ci].at[:, pl.ds(L - t_new, t_new)], sem.at[len(copies)]))
    for c in copies:
        c.start()
    for c in copies:
        c.wait()


def _roll_caches(caches, news):
    n_cache = len(caches)
    t_new = news[0].shape[1]
    any_spec = pl.BlockSpec(memory_space=pl.ANY)
    return pl.pallas_call(
        functools.partial(_roll_body, n_cache=n_cache, t_new=t_new),
        in_specs=[any_spec] * (2 * n_cache), out_specs=[any_spec] * n_cache,
        out_shape=[jax.ShapeDtypeStruct(c.shape, c.dtype) for c in caches],
        scratch_shapes=[pltpu.SemaphoreType.DMA((n_cache * (ROLL_SPLIT + 1),))],
        name="roll_caches")(*caches, *news)


def _combine_proj_body(o0, o1, o2, l0, l1, l2, x_ref, w_ref, out_ref):
    a, b, c = l0[...], l1[...], l2[...]
    m = jnp.maximum(jnp.maximum(a, b), c)
    ea, eb, ec = jnp.exp(a - m), jnp.exp(b - m), jnp.exp(c - m)
    mixed = (ea * o0[...] + eb * o1[...] + ec * o2[...]) / (ea + eb + ec)
    out_ref[...] = x_ref[...] + _dot(mixed.astype(BF16), w_ref[...])


def _combine_proj(outs, lses, x2d, w, tm):
    n = x2d.shape[0]
    full = pl.BlockSpec((tm, D_MODEL), lambda i: (i, 0))
    return pl.pallas_call(
        _combine_proj_body, grid=(n // tm,), in_specs=[full] * 7 + [pl.BlockSpec(w.shape, lambda i: (0, 0))],
        out_specs=full, out_shape=jax.ShapeDtypeStruct((n, D_MODEL), F32),
        compiler_params=_cparams("parallel"), name="combine_proj")(*outs, *lses, x2d, w)


def _top_rows(s, k, with_rank=False):
    rank = lax.broadcasted_iota(jnp.int32, (k, s.shape[1]), 0)
    top = jnp.zeros((k, s.shape[1]), F32)
    pos = jnp.full(s.shape, float(k), F32)
    for i in range(k):
        m = jnp.max(s, axis=0, keepdims=True)
        top = jnp.where(rank == i, m, top)
        hit = s == m
        if with_rank:
            pos = jnp.where(hit, float(i), pos)
        s = jnp.where(hit, NEG, s)
    return (top, pos) if with_rank else top


def _kth_largest(cand, k):
    m = None
    for _ in range(k):
        m = jnp.max(cand, axis=0, keepdims=True)
        cand = jnp.where(cand == m, NEG, cand)
    return m


def _pair_sums(a, b):
    parts = [a[0:1, :] + b]
    for i in range(1, PEER_TOPK):
        parts.append(a[i:i + 1, :] + b[0:SUBLANES, :])
    return jnp.concatenate(parts, axis=0)


def _route_body(x_ref, g_ref, wqt_ref, keys_ref, hb_ref, cnt_ref, ea_ref, rank_ref, eb_ref, qt_ref):
    hb = _rms(x_ref[...], g_ref[...]).astype(BF16)
    hb_ref[...] = hb
    qt_ref[...] = _dot_nt(wqt_ref[...], hb).astype(BF16)

    def head(h, carry):
        r1 = pl.multiple_of(h * 2 * PEER_HALF, 2 * PEER_HALF)
        r2 = pl.multiple_of(h * 2 * PEER_HALF + PEER_HALF, PEER_HALF)
        s1 = _dot(keys_ref[2 * h], qt_ref[pl.ds(r1, PEER_HALF), :])
        s2 = _dot(keys_ref[2 * h + 1], qt_ref[pl.ds(r2, PEER_HALF), :])
        a = _top_rows(s1, PEER_TOPK)
        b, rank_b = _top_rows(s2, PEER_TOPK, with_rank=True)
        cand = _pair_sums(a, b)
        tau = _kth_largest(cand, PEER_TOPK)
        mx = a[0:1, :] + b[0:1, :]
        z = jnp.sum(jnp.where(cand >= tau, jnp.exp(cand - mx), 0.0), axis=0, keepdims=True)
        cnt = jnp.zeros(s1.shape, F32)
        for jj in range(PEER_TOPK):
            cnt = cnt + jnp.where(s1 + b[jj:jj + 1, :] >= tau, 1.0, 0.0)
        cnt_ref[h] = cnt
        ea_ref[h] = jnp.exp(s1 - a[0:1, :]) / z
        rank_ref[h] = pltpu.bitcast(rank_b.astype(BF16), jnp.uint32)
        eb_ref[h] = pltpu.bitcast(jnp.exp(s2 - b[0:1, :]).astype(BF16), jnp.uint32)
        return carry

    lax.fori_loop(0, PEER_HEADS, head, 0)


def _route(x2d, gain, wqt, keys, tm):
    n = x2d.shape[0]
    tab = pl.BlockSpec((PEER_HEADS, N_KEYS, tm), lambda i: (0, 0, i))
    packed = pl.BlockSpec((PEER_HEADS, N_KEYS // 2, tm), lambda i: (0, 0, i))
    return pl.pallas_call(
        _route_body, grid=(n // tm,),
        in_specs=[pl.BlockSpec((tm, D_MODEL), lambda i: (i, 0)), pl.BlockSpec((1, D_MODEL), lambda i: (0, 0)),
                  pl.BlockSpec(wqt.shape, lambda i: (0, 0)), pl.BlockSpec(keys.shape, lambda i: (0, 0, 0))],
        out_specs=[pl.BlockSpec((tm, D_MODEL), lambda i: (i, 0)), tab, tab, packed, packed],
        out_shape=[jax.ShapeDtypeStruct((n, D_MODEL), BF16),
                   jax.ShapeDtypeStruct((PEER_HEADS, N_KEYS, n), F32),
                   jax.ShapeDtypeStruct((PEER_HEADS, N_KEYS, n), F32),
                   jax.ShapeDtypeStruct((PEER_HEADS, N_KEYS // 2, n), jnp.uint32),
                   jax.ShapeDtypeStruct((PEER_HEADS, N_KEYS // 2, n), jnp.uint32)],
        scratch_shapes=[pltpu.VMEM((D_MODEL, tm), BF16)],
        compiler_params=_cparams("parallel"), name="peer_route")(x2d, gain.reshape(1, D_MODEL), wqt, keys)


def _gelu(x):
    return 0.5 * x * (1.0 + lax.erf(x * (2.0 ** -0.5)))


def _peer_body(hb_ref, u_ref, vt_ref, cnt_ref, ea_ref, rank_ref, eb_ref, x_ref, o_ref,
               sc_a, sc_b, act_a, act_b, acc_ref, *, tm, te, n_j, n_tiles):
    s = pl.program_id(0)
    j_prev = lax.rem(jnp.clip(s - 2, 0, n_tiles - 1), n_j)

    @pl.when(s == 0)
    def _():
        for ref in (sc_a, sc_b, act_a, act_b):
            ref[...] = jnp.zeros_like(ref)

    @pl.when(j_prev == 0)
    def _():
        acc_ref[...] = jnp.zeros_like(acc_ref)

    tg = min(tm, 2 * LANES)
    zero = jnp.zeros((N_KEYS, LANES), BF16)

    def gate_unit(sc_cur, act_cur, e, c0):
        rows = slice(e * N_KEYS, (e + 1) * N_KEYS)
        prow = slice(e * N_KEYS // 2, (e + 1) * N_KEYS // 2)
        cs = slice(c0, c0 + LANES)
        g = zero
        for h in range(PEER_HEADS):
            cnt = cnt_ref[h, e:e + 1, cs].astype(BF16)
            ea = ea_ref[h, e:e + 1, cs].astype(BF16)
            rank = pltpu.bitcast(rank_ref[h, :, cs], BF16)
            eb = pltpu.bitcast(eb_ref[h, :, cs], BF16)
            g = g + jnp.where(rank < cnt, eb, zero) * ea
        act = _gelu(sc_cur[rows, cs]).astype(BF16) * g
        act_cur[prow, cs] = pltpu.bitcast(act, jnp.uint32)

    def stages(sc_new, sc_cur, act_cur, act_old):
        units = [(e, c0) for e in range(te // N_KEYS) for c0 in range(0, tm, LANES)]
        kb = 2 * LANES
        pieces = [(t0, k0) for t0 in range(0, tm, tg) for k0 in range(0, D_MODEL, kb)]
        per_piece = -(-len(units) // len(pieces))
        for p, (t0, k0) in enumerate(pieces):
            ts = slice(t0, t0 + tg)
            ks = slice(k0, k0 + kb)
            pks = slice(k0 // 2, (k0 + kb) // 2)
            acc_ref[:, ts] += _dot(vt_ref[:, ks], pltpu.bitcast(act_old[pks, ts], BF16))
            part = _dot_nt(u_ref[:, ks], hb_ref[ts, ks])
            if k0 == 0:
                sc_new[:, ts] = part
            else:
                sc_new[:, ts] += part
            for e, c0 in units[p * per_piece:(p + 1) * per_piece]:
                gate_unit(sc_cur, act_cur, e, c0)

    @pl.when(lax.rem(s, 2) == 0)
    def _():
        stages(sc_a, sc_b, act_b, act_a)

    @pl.when(lax.rem(s, 2) == 1)
    def _():
        stages(sc_b, sc_a, act_a, act_b)

    @pl.when((s >= 2) & (j_prev == n_j - 1))
    def _():
        o_ref[...] = x_ref[...] + acc_ref[...].T


def _peer_dense(hb, cnt, ea, rank, eb, x2d, u_bf, vt_bf, tm, te):
    n = x2d.shape[0]
    n_j = u_bf.shape[0] // te
    n_tiles = (n // tm) * n_j
    nxt = lambda s: jnp.minimum(s, n_tiles - 1)
    cur = lambda s: jnp.clip(s - 1, 0, n_tiles - 1)
    prv = lambda s: jnp.clip(s - 2, 0, n_tiles - 1)
    first = pl.BlockSpec((PEER_HEADS, te // N_KEYS, tm), lambda s: (0, cur(s) % n_j, cur(s) // n_j))
    second = pl.BlockSpec((PEER_HEADS, N_KEYS // 2, tm), lambda s: (0, 0, cur(s) // n_j))
    return pl.pallas_call(
        functools.partial(_peer_body, tm=tm, te=te, n_j=n_j, n_tiles=n_tiles), grid=(n_tiles + 2,),
        in_specs=[pl.BlockSpec((tm, D_MODEL), lambda s: (nxt(s) // n_j, 0)),
                  pl.BlockSpec((te, D_MODEL), lambda s: (nxt(s) % n_j, 0)),
                  pl.BlockSpec((D_MODEL, te), lambda s: (0, prv(s) % n_j)),
                  first, first, second, second,
                  pl.BlockSpec((tm, D_MODEL), lambda s: (prv(s) // n_j, 0))],
        out_specs=pl.BlockSpec((tm, D_MODEL), lambda s: (prv(s) // n_j, 0)),
        out_shape=jax.ShapeDtypeStruct((n, D_MODEL), F32),
        scratch_shapes=[pltpu.VMEM((te, tm), F32), pltpu.VMEM((te, tm), F32),
                        pltpu.VMEM((te // 2, tm), jnp.uint32), pltpu.VMEM((te // 2, tm), jnp.uint32),
                        pltpu.VMEM((D_MODEL, tm), F32)],
        compiler_params=_cparams("arbitrary"), name="peer_dense")(
            hb, u_bf, vt_bf, cnt, ea, rank, eb, x2d)


def _ple_body(x_ref, p_ref, g_ref, wg_ref, wp_ref, gf_ref, o_ref, *, final_norm):
    x = x_ref[...]
    gate = _sigmoid(_dot(_rms(x, g_ref[...]).astype(BF16), wg_ref[...]))
    emb = _dot(p_ref[...].astype(BF16), wp_ref[...])
    y = x + gate * emb
    if final_norm:
        y = _rms(y, gf_ref[...])
    o_ref[...] = y


def _ple(x2d, p2d, gain, wg, wp, gain_final, final_norm, tm):
    n = x2d.shape[0]
    full = pl.BlockSpec((tm, D_MODEL), lambda i: (i, 0))
    vec = pl.BlockSpec((1, D_MODEL), lambda i: (0, 0))
    return pl.pallas_call(
        functools.partial(_ple_body, final_norm=final_norm), grid=(n // tm,),
        in_specs=[full, pl.BlockSpec((tm, PLE_DIM), lambda i: (i, 0)), vec,
                  pl.BlockSpec(wg.shape, lambda i: (0, 0)), pl.BlockSpec(wp.shape, lambda i: (0, 0)), vec],
        out_specs=full, out_shape=jax.ShapeDtypeStruct((n, D_MODEL), F32),
        compiler_params=_cparams("parallel"), name="ple")(
            x2d, p2d, gain.reshape(1, D_MODEL), wg, wp, gain_final.reshape(1, D_MODEL))


PEER_TE = 1024


def _tail(x2d, p2d, norm_ffn, wqt, keys, u_bf, vt_bf, norm_ple, wg, wp, norm_final, final_norm, tm_route, tm):
    hb, cnt, ea, rank, eb = _route(x2d, norm_ffn, wqt, keys, tm_route)
    x2 = _peer_dense(hb, cnt, ea, rank, eb, x2d, u_bf, vt_bf, tm, PEER_TE)
    return _ple(x2, p2d, norm_ple, wg, wp, norm_final, final_norm, tm)


def kernel(x_prompt, x_sample, state_conv_a, state_conv_qkv, state_delta, cache_kv_w128, cache_kv_w512,
           cache_kv_w2048, p_prompt, p_sample, norm_mix, norm_ffn, norm_ple, norm_final, w_in_ab, conv_a_w,
           conv_a_b, ln_a_g, ln_a_b, conv_qkv_w, a_log, dt_bias, gdn_norm_g, w_out_ab, w_in_c, w_out_c,
           peer_wq, peer_keys, peer_u, peer_v, ple_gate, ple_proj):
    bp, s, _ = x_prompt.shape
    bs, t_new, _ = x_sample.shape
    n_p = bp * s
    n_s = bs * t_new
    nhd = C_HEADS * C_HD
    tm_p = 512
    tm_route_p = 256

    o1 = 2 * D_A
    o2 = o1 + QKV_W
    o3 = o2 + GDN_HEADS * GDN_DV
    w_ab = w_in_ab[0].astype(BF16)
    w_bg = jnp.pad(w_ab[:, o3:], ((0, 0), (0, LANES - 2 * GDN_HEADS)))
    w_ab_parts = [w_ab[:, :o1], w_ab[:, o1:o2], w_ab[:, o2:o3], w_bg]
    w_out_a = w_out_ab[0, :D_A].astype(BF16)
    w_out_b = w_out_ab[0, D_A:].astype(BF16)
    w_c = w_in_c[0].astype(BF16)
    n_g = len(C_PATTERNS)
    w_q_all = w_c[:, :n_g * nhd]
    w_kv = [jnp.concatenate([w_c[:, (n_g + g) * nhd:(n_g + g + 1) * nhd],
                             w_c[:, (2 * n_g + g) * nhd:(2 * n_g + g + 1) * nhd]], axis=1) for g in range(n_g)]
    w_oc = w_out_c[0].astype(BF16)
    wqt = [peer_wq[i].T.astype(BF16) for i in range(2)]
    keys = [peer_keys[i].reshape(2 * PEER_HEADS, N_KEYS, PEER_HALF).astype(BF16) for i in range(2)]
    u_bf = [peer_u[i].astype(BF16) for i in range(2)]
    vt_bf = [peer_v[i].astype(BF16).T for i in range(2)]
    wg = [ple_gate[i].astype(BF16) for i in range(2)]
    wp = [ple_proj[i].astype(BF16) for i in range(2)]

    t_pad = GDN_CHUNK
    xs_pad = jnp.pad(x_sample, ((0, 0), (0, t_pad - t_new), (0, 0)))
    hist_a_s = jnp.pad(state_conv_a[0], ((0, 0), (CONV_HIST - (CONV_A_WIDTH - 1), 0), (0, 0)))
    hist_q_s = jnp.pad(state_conv_qkv[0], ((0, 0), (SUBLANES - (GDN_CONV - 1), 0), (0, 0)))

    def layer0(x3d, hist_a, hist_q, s0, t_valid, ts):
        b, t, _ = x3d.shape
        x2d = x3d.reshape(b * t, D_MODEL)
        zglu, zqkv, zgate, zbg = _norm_mm(x2d, norm_mix[0], w_ab_parts, 512)
        ya, glu = _conv_a(zglu.reshape(b, t, 2 * D_A), hist_a, conv_a_w[0], conv_a_b[0], ln_a_g[0], ln_a_b[0], ts)
        zqkv3 = zqkv.reshape(b, t, QKV_W)
        od, s_fin = _gdn(zqkv3, zgate.reshape(b, t, -1), zbg.reshape(b, t, LANES), hist_q, s0, conv_qkv_w[0],
                         a_log[0], dt_bias[0], gdn_norm_g[0], t_valid)
        x1 = _out_proj_ab(ya.reshape(b * t, D_A), od.reshape(b * t, -1), x2d, w_out_a, w_out_b, 512)
        return x1.reshape(b, t, D_MODEL), glu, zqkv3, s_fin

    xp1, glu_p, zqkv_p, sfin_p = layer0(
        x_prompt, jnp.zeros((bp, CONV_HIST, D_A), F32), jnp.zeros((bp, SUBLANES, QKV_W), F32),
        jnp.zeros((bp, GDN_HEADS, GDN_DK, GDN_DV), F32), s, 256)
    xs1, glu_s, zqkv_s, sfin_s = layer0(xs_pad, hist_a_s, hist_q_s, state_delta[0], t_new, t_pad)
    xs1 = xs1[:, :t_new]

    na = CONV_A_WIDTH - 1
    nq = GDN_CONV - 1
    new_a_p = glu_p[:, s - na:][None]
    new_q_p = zqkv_p[:, s - nq:][None]
    new_a_s = jnp.concatenate([state_conv_a[0], glu_s[:, :t_new]], axis=1)[:, t_new:][None]
    new_q_s = jnp.concatenate([state_conv_qkv[0], zqkv_s[:, :t_new]], axis=1)[:, t_new:][None]

    xp2 = _tail(xp1.reshape(n_p, D_MODEL), p_prompt[0].reshape(n_p, PLE_DIM), norm_ffn[0], wqt[0], keys[0], u_bf[0],
                vt_bf[0], norm_ple[0], wg[0], wp[0], norm_final, False, tm_route_p, tm_p)
    xs2 = _tail(xs1.reshape(n_s, D_MODEL), p_sample[0].reshape(n_s, PLE_DIM), norm_ffn[0], wqt[0], keys[0], u_bf[0],
                vt_bf[0], norm_ple[0], wg[0], wp[0], norm_final, False, n_s, n_s)

    tab_p = _rope_tables(jnp.arange(s))
    tab_s = _rope_tables(PAST_LEN + (jnp.arange(n_s) % t_new))
    q_p = _norm_mm_rope(xp2, norm_mix[1], tab_p, w_q_all, n_g * C_HEADS, tm_p)
    q_s = _norm_mm_rope(xs2, norm_mix[1], tab_s, w_q_all, n_g * C_HEADS, n_s)
    caches = (cache_kv_w128[0], cache_kv_w512[0], cache_kv_w2048[0])
    q_s5 = q_s.reshape(bs, t_new, n_g, C_HEADS, C_HD)
    outs_p, lses_p, outs_s, lses_s, kv_p, kv_new = [], [], [], [], [], []
    for g, (win, dil) in enumerate(C_PATTERNS):
        kvp = _norm_mm_rope(xp2, norm_mix[1], tab_p, w_kv[g], C_HEADS, tm_p)
        o, l = _band_attn(q_p, kvp, g, dil, bp, s)
        outs_p.append(o)
        lses_p.append(l)
        keep = min(win, s)
        kv_p.append(kvp.reshape(bp, s, 2, C_HEADS, C_HD)[:, s - keep:][None])

        kvs = _norm_mm_rope(xs2, norm_mix[1], tab_s, w_kv[g], C_HEADS, n_s).reshape(bs, t_new, 2, C_HEADS, C_HD)
        o, l = _gather_attn(q_s5[:, :, g], caches[g], kvs, dil)
        outs_s.append(o.reshape(n_s, nhd))
        lses_s.append(l.reshape(n_s, nhd))
        kv_new.append(kvs)
    kv_s = [c[None] for c in _roll_caches(caches, kv_new)]

    xp3 = _combine_proj(outs_p, lses_p, xp2, w_oc, tm_p)
    xs3 = _combine_proj(outs_s, lses_s, xs2, w_oc, n_s)
    y_p = _tail(xp3, p_prompt[1].reshape(n_p, PLE_DIM), norm_ffn[1], wqt[1], keys[1], u_bf[1], vt_bf[1],
                norm_ple[1], wg[1], wp[1], norm_final, True, tm_route_p, tm_p)
    y_s = _tail(xs3, p_sample[1].reshape(n_s, PLE_DIM), norm_ffn[1], wqt[1], keys[1], u_bf[1], vt_bf[1],
                norm_ple[1], wg[1], wp[1], norm_final, True, n_s, n_s)

    return (y_p.reshape(bp, s, D_MODEL), y_s.reshape(bs, t_new, D_MODEL),
            new_a_p, new_q_p, sfin_p[None], kv_p[0], kv_p[1], kv_p[2],
            new_a_s, new_q_s, sfin_s[None], kv_s[0], kv_s[1], kv_s[2])
```

```python
import functools
import math

import jax
import jax.numpy as jnp
import numpy as np
from jax import lax
from jax.experimental import pallas as pl
from jax.experimental.pallas import tpu as pltpu

F32 = jnp.float32
BF16 = jnp.bfloat16

D_MODEL = 1024
EPS = 1e-6
D_A = 512
CONV_A_WIDTH = 31
GDN_HEADS = 4
GDN_DK = 128
GDN_DV = 128
GDN_CONV = 4
GDN_CHUNK = 64
QKV_W = GDN_HEADS * (2 * GDN_DK + GDN_DV)
C_PATTERNS = ((128, 1), (512, 4), (2048, 16))
C_HEADS = 8
C_HD = 128
ROT_DIM = 32
ROPE_THETA = 500000.0
ATT_BLOCK = 128
PAST_LEN = 8192
PEER_HEADS = 8
N_KEYS = 128
PEER_TOPK = 16
PEER_HALF = 64
PLE_DIM = 256

LANES = 128
SUBLANES = 8
VMEM_LIMIT_BYTES = 56 * 1024 * 1024

NEG = -1e30
LOG2E = 1.4426950408889634


def _cparams(*sem):
    return pltpu.CompilerParams(dimension_semantics=sem, vmem_limit_bytes=VMEM_LIMIT_BYTES)


def _rms(x, g):
    return x * lax.rsqrt(jnp.mean(x * x, axis=-1, keepdims=True) + EPS) * g


def _sigmoid(x):
    return 1.0 / (1.0 + jnp.exp(-x))


def _dot(a, b):
    return jnp.dot(a, b, preferred_element_type=F32)


def _dot_nt(a, b):
    return lax.dot_general(a, b, (((1,), (1,)), ((), ())), preferred_element_type=F32)


def _dot_tn(a, b):
    return lax.dot_general(a, b, (((0,), (0,)), ((), ())), preferred_element_type=F32)


def _norm_mm_body(x_ref, g_ref, *refs):
    nw = len(refs) // 2
    h = _rms(x_ref[...], g_ref[...]).astype(BF16)
    for w_ref, o_ref in zip(refs[:nw], refs[nw:]):
        o_ref[...] = _dot(h, w_ref[...])


def _norm_mm(x2d, gain, ws, tm):
    n = x2d.shape[0]
    in_specs = [pl.BlockSpec((tm, D_MODEL), lambda i: (i, 0)), pl.BlockSpec((1, D_MODEL), lambda i: (0, 0))]
    in_specs += [pl.BlockSpec(w.shape, lambda i: (0, 0)) for w in ws]
    out_specs = [pl.BlockSpec((tm, w.shape[1]), lambda i: (i, 0)) for w in ws]
    out_shape = [jax.ShapeDtypeStruct((n, w.shape[1]), F32) for w in ws]
    return pl.pallas_call(
        _norm_mm_body, grid=(n // tm,), in_specs=in_specs, out_specs=out_specs, out_shape=out_shape,
        compiler_params=_cparams("parallel"), name="norm_mm")(x2d, gain.reshape(1, D_MODEL), *ws)


def _norm_mm_rope_body(x_ref, g_ref, cos_ref, sa_ref, sb_ref, w_ref, o_ref, *, n_rope, n_heads):
    h = _rms(x_ref[...], g_ref[...]).astype(BF16)
    cos = cos_ref[...]
    sa = sa_ref[...]
    sb = sb_ref[...]
    group = 2 * LANES
    for c0 in range(0, n_heads * C_HD, group):
        og = _dot(h, w_ref[:, c0:c0 + group])
        for hh in range(c0 // C_HD, (c0 + group) // C_HD):
            o = og[:, hh * C_HD - c0:(hh + 1) * C_HD - c0]
            if hh < n_rope:
                half = ROT_DIM // 2
                o = o * cos + pltpu.roll(o, half, 1) * sa + pltpu.roll(o, C_HD - half, 1) * sb
            o_ref[:, hh * C_HD:(hh + 1) * C_HD] = o


def _rope_tables(pos):
    half = ROT_DIM // 2
    inv = 1.0 / (ROPE_THETA ** (jnp.arange(0, ROT_DIM, 2, dtype=F32) / ROT_DIM))
    ang = pos.astype(F32)[:, None] * inv[None, :]
    c, s = jnp.cos(ang), jnp.sin(ang)
    p = pos.shape[0]
    ones = jnp.ones((p, C_HD - ROT_DIM), F32)
    zeros = jnp.zeros((p, C_HD - ROT_DIM), F32)
    zh = jnp.zeros((p, half), F32)
    cos = jnp.concatenate([c, c, ones], axis=1)
    sa = jnp.concatenate([zh, s, zeros], axis=1)
    sb = jnp.concatenate([-s, zh, zeros], axis=1)
    return cos, sa, sb


def _norm_mm_rope(x2d, gain, tables, w, n_rope, tm):
    n = x2d.shape[0]
    ncol = w.shape[1]
    cos, sa, sb = tables
    pblk = cos.shape[0] // tm
    tspec = pl.BlockSpec((tm, C_HD), lambda i: (i % pblk, 0))
    return pl.pallas_call(
        functools.partial(_norm_mm_rope_body, n_rope=n_rope, n_heads=ncol // C_HD),
        grid=(n // tm,),
        in_specs=[pl.BlockSpec((tm, D_MODEL), lambda i: (i, 0)), pl.BlockSpec((1, D_MODEL), lambda i: (0, 0)),
                  tspec, tspec, tspec, pl.BlockSpec(w.shape, lambda i: (0, 0))],
        out_specs=pl.BlockSpec((tm, ncol), lambda i: (i, 0)),
        out_shape=jax.ShapeDtypeStruct((n, ncol), F32),
        compiler_params=_cparams("parallel"), name="norm_mm_rope")(x2d, gain.reshape(1, D_MODEL), cos, sa, sb, w)


CONV_HIST = 32


def _conv_a_body(z_ref, hist_ref, w_ref, b_ref, lg_ref, lb_ref, ya_ref, glu_ref, ext_ref, *, ts):
    t = pl.program_id(1)

    @pl.when(t == 0)
    def _():
        ext_ref[0:CONV_HIST, :] = hist_ref[0]

    @pl.when(t > 0)
    def _():
        ext_ref[0:CONV_HIST, :] = ext_ref[ts:ts + CONV_HIST, :]

    z = z_ref[0]
    glu = z[:, :D_A] * _sigmoid(z[:, D_A:])
    glu_ref[0] = glu
    ext_ref[CONV_HIST:CONV_HIST + ts, :] = glu
    ch = min(ts, 64)
    first = CONV_HIST - (CONV_A_WIDTH - 1)
    for c in range(ts // ch):
        acc = jnp.zeros((ch, D_A), F32)
        for j in range(CONV_A_WIDTH):
            r0 = first + j + c * ch
            acc = acc + w_ref[j:j + 1, :] * ext_ref[r0:r0 + ch, :]
        y = acc + b_ref[...]
        yc = y - jnp.mean(y, axis=-1, keepdims=True)
        ln = yc * lax.rsqrt(jnp.mean(yc * yc, axis=-1, keepdims=True) + EPS) * lg_ref[...] + lb_ref[...]
        ya_ref[0, c * ch:(c + 1) * ch, :] = ln * _sigmoid(ln)


def _conv_a(zglu, hist, conv_w, conv_b, ln_g, ln_b, ts):
    b, t, _ = zglu.shape
    row = lambda a: a.reshape(1, D_A)
    wpad = jnp.pad(conv_w, ((0, CONV_HIST - CONV_A_WIDTH), (0, 0)))
    vec = pl.BlockSpec((1, D_A), lambda i, j: (0, 0))
    return pl.pallas_call(
        functools.partial(_conv_a_body, ts=ts),
        grid=(b, t // ts),
        in_specs=[pl.BlockSpec((1, ts, 2 * D_A), lambda i, j: (i, j, 0)),
                  pl.BlockSpec((1, CONV_HIST, D_A), lambda i, j: (i, 0, 0)),
                  pl.BlockSpec((CONV_HIST, D_A), lambda i, j: (0, 0)), vec, vec, vec],
        out_specs=[pl.BlockSpec((1, ts, D_A), lambda i, j: (i, j, 0)),
                   pl.BlockSpec((1, ts, D_A), lambda i, j: (i, j, 0))],
        out_shape=[jax.ShapeDtypeStruct((b, t, D_A), F32), jax.ShapeDtypeStruct((b, t, D_A), F32)],
        scratch_shapes=[pltpu.VMEM((CONV_HIST + ts, D_A), F32)],
        compiler_params=_cparams("arbitrary", "arbitrary"), name="conv_a")(
            zglu, hist, wpad, row(conv_b), row(ln_g), row(ln_b))


def _gdn_body(zq_ref, zg_ref, zb_ref, hist_ref, s0_ref, cw_ref, alog_ref, dtb_ref, ng_ref,
              od_ref, s_ref, ext_ref, *, t_valid):
    c = pl.program_id(1)
    ck = GDN_CHUNK

    @pl.when(c == 0)
    def _():
        ext_ref[0:SUBLANES, :] = hist_ref[0]
        s_ref[0] = s0_ref[0]

    @pl.when(c > 0)
    def _():
        ext_ref[0:SUBLANES, :] = ext_ref[ck:ck + SUBLANES, :]

    ext_ref[SUBLANES:SUBLANES + ck, :] = zq_ref[0]
    conv = jnp.zeros((ck, QKV_W), F32)
    for j in range(GDN_CONV):
        r0 = SUBLANES - (GDN_CONV - 1) + j
        conv = conv + cw_ref[j:j + 1, :] * ext_ref[r0:r0 + ck, :]
    qkv = conv * _sigmoid(conv)

    zb = zb_ref[0]
    row = c * ck + lax.broadcasted_iota(jnp.int32, (ck, LANES), 0)
    valid = row < t_valid
    beta_all = jnp.where(valid, _sigmoid(zb), 0.0)
    xg = zb + dtb_ref[...]
    softplus = jnp.maximum(xg, 0.0) + jnp.log(1.0 + jnp.exp(-jnp.abs(xg)))
    g_all = jnp.where(valid, -jnp.exp(alog_ref[...]) * softplus, 0.0)

    ri = lax.broadcasted_iota(jnp.int32, (ck, ck), 0)
    ci = lax.broadcasted_iota(jnp.int32, (ck, ck), 1)
    causal = ri >= ci
    strict = ri > ci
    tri = causal.astype(F32)
    eye = (ri == ci).astype(F32)
    ri2 = lax.broadcasted_iota(jnp.int32, (ck, 2 * ck), 0)
    ci2 = lax.broadcasted_iota(jnp.int32, (ck, 2 * ck), 1)
    upper2 = ((ri2 > ci2) | (ci2 == ck)).astype(F32)

    nq = GDN_HEADS * GDN_DK
    heads = range(GDN_HEADS)
    q = [qkv[:, h * GDN_DK:(h + 1) * GDN_DK] for h in heads]
    k = [qkv[:, nq + h * GDN_DK:nq + (h + 1) * GDN_DK] for h in heads]
    v = [qkv[:, 2 * nq + h * GDN_DV:2 * nq + (h + 1) * GDN_DV] for h in heads]
    q = [x * lax.rsqrt(jnp.sum(x * x, axis=-1, keepdims=True) + EPS) * (GDN_DK ** -0.5) for x in q]
    k = [x * lax.rsqrt(jnp.sum(x * x, axis=-1, keepdims=True) + EPS) for x in k]
    beta = [beta_all[:, h:h + 1] for h in heads]
    g = [g_all[:, GDN_HEADS + h:GDN_HEADS + h + 1] for h in heads]
    dm = [jnp.dot(tri, g[h] * upper2, preferred_element_type=F32, precision=lax.Precision.HIGHEST) for h in heads]
    gcum = [x[:, ck:ck + 1] for x in dm]
    glast = [x[ck - 1:ck, :] for x in gcum]
    decay = [jnp.where(causal, jnp.exp(x[:, :ck]), 0.0) for x in dm]
    egc = [jnp.exp(x) for x in gcum]
    kb = [k[h] * beta[h] for h in heads]
    p = [-jnp.where(strict, _dot_nt(kb[h], k[h]) * decay[h], 0.0) for h in heads]
    tinv = [eye + x for x in p]
    for _ in range(int(math.log2(ck)) - 1):
        p = [_dot(x, x) for x in p]
        tinv = [tinv[h] + _dot(tinv[h], p[h]) for h in heads]
    u = [_dot(tinv[h], v[h] * beta[h]) for h in heads]
    w = [_dot(tinv[h], kb[h] * egc[h]) for h in heads]
    intra = [jnp.where(causal, _dot_nt(q[h], k[h]) * decay[h], 0.0) for h in heads]
    s_old = [s_ref[0, h] for h in heads]
    vnew = [u[h] - _dot(w[h], s_old[h]) for h in heads]
    o = [_dot(q[h] * egc[h], s_old[h]) + _dot(intra[h], vnew[h]) for h in heads]
    for h in heads:
        s_ref[0, h] = s_old[h] * jnp.exp(glast[h]) + _dot_tn(k[h] * jnp.exp(glast[h] - gcum[h]), vnew[h])
    for h in heads:
        gate = zg_ref[0, :, h * GDN_DV:(h + 1) * GDN_DV]
        on = o[h] * lax.rsqrt(jnp.mean(o[h] * o[h], axis=-1, keepdims=True) + EPS) * ng_ref[...]
        od_ref[0, :, h * GDN_DV:(h + 1) * GDN_DV] = on * (gate * _sigmoid(gate))


def _gdn(zqkv, zgate, zbg, hist, s0, conv_w, a_log, dt_bias, norm_g, t_valid):
    b, t, _ = zqkv.shape
    ck = GDN_CHUNK
    lane_vec = lambda a: jnp.zeros((1, LANES), F32).at[0, GDN_HEADS:2 * GDN_HEADS].set(a)
    return pl.pallas_call(
        functools.partial(_gdn_body, t_valid=t_valid),
        grid=(b, t // ck),
        in_specs=[pl.BlockSpec((1, ck, QKV_W), lambda i, j: (i, j, 0)),
                  pl.BlockSpec((1, ck, GDN_HEADS * GDN_DV), lambda i, j: (i, j, 0)),
                  pl.BlockSpec((1, ck, LANES), lambda i, j: (i, j, 0)),
                  pl.BlockSpec((1, SUBLANES, QKV_W), lambda i, j: (i, 0, 0)),
                  pl.BlockSpec((1, GDN_HEADS, GDN_DK, GDN_DV), lambda i, j: (i, 0, 0, 0)),
                  pl.BlockSpec((GDN_CONV, QKV_W), lambda i, j: (0, 0)),
                  pl.BlockSpec((1, LANES), lambda i, j: (0, 0)),
                  pl.BlockSpec((1, LANES), lambda i, j: (0, 0)),
                  pl.BlockSpec((1, GDN_DV), lambda i, j: (0, 0))],
        out_specs=[pl.BlockSpec((1, ck, GDN_HEADS * GDN_DV), lambda i, j: (i, j, 0)),
                   pl.BlockSpec((1, GDN_HEADS, GDN_DK, GDN_DV), lambda i, j: (i, 0, 0, 0))],
        out_shape=[jax.ShapeDtypeStruct((b, t, GDN_HEADS * GDN_DV), F32),
                   jax.ShapeDtypeStruct((b, GDN_HEADS, GDN_DK, GDN_DV), F32)],
        scratch_shapes=[pltpu.VMEM((SUBLANES + ck, QKV_W), F32)],
        compiler_params=_cparams("arbitrary", "arbitrary"), name="gdn")(
            zqkv, zgate, zbg, hist, s0, conv_w, lane_vec(a_log), lane_vec(dt_bias), norm_g.reshape(1, GDN_DV))


def _out_proj_ab_body(ya_ref, od_ref, x_ref, wa_ref, wb_ref, o_ref):
    y = _dot(ya_ref[...].astype(BF16), wa_ref[...]) + _dot(od_ref[...].astype(BF16), wb_ref[...])
    o_ref[...] = x_ref[...] + y


def _out_proj_ab(ya, od, x2d, wa, wb, tm):
    n = x2d.shape[0]
    half = pl.BlockSpec((tm, D_A), lambda i: (i, 0))
    full = pl.BlockSpec((tm, D_MODEL), lambda i: (i, 0))
    wspec = pl.BlockSpec((D_A, D_MODEL), lambda i: (0, 0))
    return pl.pallas_call(
        _out_proj_ab_body, grid=(n // tm,), in_specs=[half, half, full, wspec, wspec], out_specs=full,
        out_shape=jax.ShapeDtypeStruct((n, D_MODEL), F32), compiler_params=_cparams("parallel"),
        name="out_proj_ab")(ya, od, x2d, wa, wb)


def _band_streams(blk, q, kc, vc, kp, vp):
    n = range(len(q))
    ri = lax.broadcasted_iota(jnp.int32, (ATT_BLOCK, ATT_BLOCK), 0)
    ci = lax.broadcasted_iota(jnp.int32, (ATT_BLOCK, ATT_BLOCK), 1)
    cur_ok = ci <= ri
    prev_ok = (ci >= ri) & (blk > 0)
    scale = C_HD ** -0.5
    s_c = [jnp.where(cur_ok, _dot_nt(q[i], kc[i]) * scale, NEG) for i in n]
    m = [jnp.max(x, axis=-1, keepdims=True) for x in s_c]
    if kp is not None:
        s_p = [jnp.where(prev_ok, _dot_nt(q[i], kp[i]) * scale, NEG) for i in n]
        m = [jnp.maximum(m[i], jnp.max(s_p[i], axis=-1, keepdims=True)) for i in n]
    p_c = [jnp.exp(s_c[i] - m[i]) for i in n]
    den = [jnp.sum(x, axis=-1, keepdims=True) for x in p_c]
    o = [_dot(p_c[i].astype(BF16), vc[i]) for i in n]
    if kp is not None:
        p_p = [jnp.exp(s_p[i] - m[i]) for i in n]
        den = [den[i] + jnp.sum(p_p[i], axis=-1, keepdims=True) for i in n]
        o = [o[i] + _dot(p_p[i].astype(BF16), vp[i]) for i in n]
    outs = [o[i] / den[i] for i in n]
    lses = [jnp.broadcast_to(m[i] + jnp.log(den[i]), (ATT_BLOCK, C_HD)) for i in n]
    return outs, lses


def _band_attn_body(q_ref, kc_ref, kp_ref, o_ref, l_ref):
    nhd = C_HEADS * C_HD
    cols = [slice(h * C_HD, (h + 1) * C_HD) for h in range(C_HEADS)]
    vcols = [slice(nhd + h * C_HD, nhd + (h + 1) * C_HD) for h in range(C_HEADS)]
    outs, lses = _band_streams(
        pl.program_id(1),
        [q_ref[0, :, c].astype(BF16) for c in cols],
        [kc_ref[0, :, c].astype(BF16) for c in cols], [kc_ref[0, :, c].astype(BF16) for c in vcols],
        [kp_ref[0, :, c].astype(BF16) for c in cols], [kp_ref[0, :, c].astype(BF16) for c in vcols])
    for h, c in enumerate(cols):
        o_ref[0, :, c] = outs[h]
        l_ref[0, :, c] = lses[h]


BAND_LOCKSTEP = 4


def _band_attn_dilated_body(q_ref, k_ref, v_ref, kp_ref, vp_ref, o_ref, l_ref, *, dil, has_prev):
    blk = pl.program_id(1)

    def stream_group(gidx, carry):
        rows = [pl.ds(gidx * BAND_LOCKSTEP + i, ATT_BLOCK, stride=dil) for i in range(BAND_LOCKSTEP)]
        load = lambda ref: [ref[0, r, :].astype(BF16) for r in rows]
        outs, lses = _band_streams(blk, load(q_ref), load(k_ref), load(v_ref),
                                   load(kp_ref) if has_prev else None, load(vp_ref) if has_prev else None)
        for i, r in enumerate(rows):
            o_ref[0, r, :] = outs[i]
            l_ref[0, r, :] = lses[i]
        return carry

    lax.fori_loop(0, dil // BAND_LOCKSTEP, stream_group, 0)


def _band_attn(q_all, kv, gi, dil, b, s):
    nhd = C_HEADS * C_HD
    n_groups = len(C_PATTERNS)
    q3 = q_all.reshape(b, s, n_groups * nhd)
    kv3 = kv.reshape(b, s, 2 * nhd)
    if dil == 1:
        nblk = s // ATT_BLOCK
        ospec = pl.BlockSpec((1, ATT_BLOCK, nhd), lambda i, j: (i, j, 0))
        o, l = pl.pallas_call(
            _band_attn_body, grid=(b, nblk),
            in_specs=[pl.BlockSpec((1, ATT_BLOCK, nhd), lambda i, j: (i, j, gi)),
                      pl.BlockSpec((1, ATT_BLOCK, 2 * nhd), lambda i, j: (i, j, 0)),
                      pl.BlockSpec((1, ATT_BLOCK, 2 * nhd), lambda i, j: (i, jnp.maximum(j - 1, 0), 0))],
            out_specs=[ospec, ospec],
            out_shape=[jax.ShapeDtypeStruct((b, s, nhd), F32)] * 2,
            compiler_params=_cparams("parallel", "arbitrary"), name="band_attn")(q3, kv3, kv3)
    else:
        rows = dil * ATT_BLOCK
        nblk = s // rows
        cur = lambda c0: pl.BlockSpec((1, rows, C_HD), lambda i, j, h: (i, j, c0 + h))
        prev = lambda c0: pl.BlockSpec((1, rows, C_HD), lambda i, j, h: (i, jnp.maximum(j - 1, 0), c0 + h))
        o, l = pl.pallas_call(
            functools.partial(_band_attn_dilated_body, dil=dil, has_prev=nblk > 1), grid=(b, nblk, C_HEADS),
            in_specs=[cur(gi * C_HEADS), cur(0), cur(C_HEADS), prev(0), prev(C_HEADS)],
            out_specs=[cur(0), cur(0)],
            out_shape=[jax.ShapeDtypeStruct((b, s, nhd), F32)] * 2,
            compiler_params=_cparams("parallel", "arbitrary", "arbitrary"), name="band_attn_dilated")(
                q3, kv3, kv3, kv3, kv3)
    return o.reshape(b * s, nhd), l.reshape(b * s, nhd)


def _gather_attn_body(q_ref, c_ref, n_ref, ones_ref, o_ref, l_ref, *, dil, t_new):
    scale = C_HD ** -0.5
    rows = c_ref.shape[1]
    crow = lax.broadcasted_iota(jnp.int32, (rows, C_HEADS, C_HD), 0)
    nrow = lax.broadcasted_iota(jnp.int32, (t_new, C_HEADS, C_HD), 0)

    def head_dots(k, q):
        prod = (k * q[None]).astype(BF16).reshape(k.shape[0] * C_HEADS, C_HD)
        return (_dot(prod, ones_ref[...]) * scale).reshape(k.shape)

    for t in range(t_new):
        r = 0 if dil == 1 else t
        q = q_ref[0, t]
        s_c = head_dots(c_ref[0, :, r, 0], q)
        s_n = head_dots(n_ref[0, :, 0], q)
        if dil == 1:
            s_c = jnp.where(crow >= t, s_c, NEG)
            s_n = jnp.where(nrow <= t, s_n, NEG)
        else:
            s_n = jnp.where(nrow == t, s_n, NEG)
        m = jnp.maximum(jnp.max(s_c, axis=0), jnp.max(s_n, axis=0))
        p_c = jnp.exp(s_c - m[None])
        p_n = jnp.exp(s_n - m[None])
        den = jnp.sum(p_c, axis=0) + jnp.sum(p_n, axis=0)
        o = jnp.sum(p_c * c_ref[0, :, r, 1], axis=0) + jnp.sum(p_n * n_ref[0, :, 1], axis=0)
        o_ref[0, t] = o / den
        l_ref[0, t] = m + jnp.log(den)


def _gather_attn(q_g, cache, kv_new, dil):
    b, t_new = q_g.shape[:2]
    L = cache.shape[1]
    rows = L // dil
    streams = 1 if dil == 1 else t_new
    cv = cache.reshape(b, rows, dil, 2, C_HEADS, C_HD)
    ones = jnp.ones((C_HD, C_HD), BF16)
    ospec = pl.BlockSpec((1, t_new, C_HEADS, C_HD), lambda i: (i, 0, 0, 0))
    return pl.pallas_call(
        functools.partial(_gather_attn_body, dil=dil, t_new=t_new), grid=(b,),
        in_specs=[ospec,
                  pl.BlockSpec((1, rows, streams, 2, C_HEADS, C_HD), lambda i: (i, 0, 0, 0, 0, 0)),
                  pl.BlockSpec((1, t_new, 2, C_HEADS, C_HD), lambda i: (i, 0, 0, 0, 0)),
                  pl.BlockSpec(ones.shape, lambda i: (0, 0))],
        out_specs=[ospec, ospec],
        out_shape=[jax.ShapeDtypeStruct((b, t_new, C_HEADS, C_HD), F32)] * 2,
        compiler_params=_cparams("parallel"), name="gather_attn")(q_g, cv, kv_new, ones)


ROLL_ROWS = 256


def _roll_body(cur_ref, nxt_ref, new_ref, o_ref, *, t_new):
    j = pl.program_id(1)
    rows = cur_ref.shape[1]
    o_ref[0, 0:rows - t_new] = cur_ref[0, t_new:rows]

    @pl.when(j < pl.num_programs(1) - 1)
    def _():
        o_ref[0, rows - t_new:rows] = nxt_ref[0]

    @pl.when(j == pl.num_programs(1) - 1)
    def _():
        o_ref[0, rows - t_new:rows] = new_ref[0]


def _roll_cache(cache, new):
    b, L = cache.shape[:2]
    t_new = new.shape[1]
    rows = min(L, ROLL_ROWS)
    per = rows // t_new
    last = L // t_new - 1
    tail = cache.shape[2:]
    zeros = (0,) * len(tail)
    return pl.pallas_call(
        functools.partial(_roll_body, t_new=t_new), grid=(b, L // rows),
        in_specs=[pl.BlockSpec((1, rows) + tail, lambda i, j: (i, j) + zeros),
                  pl.BlockSpec((1, t_new) + tail, lambda i, j: (i, jnp.minimum((j + 1) * per, last)) + zeros),
                  pl.BlockSpec((1, t_new) + tail, lambda i, j: (i, 0) + zeros)],
        out_specs=pl.BlockSpec((1, rows) + tail, lambda i, j: (i, j) + zeros),
        out_shape=jax.ShapeDtypeStruct(cache.shape, cache.dtype),
        compiler_params=_cparams("parallel", "arbitrary"), name="roll_cache")(cache, cache, new)


def _roll_caches(caches, news):
    return [_roll_cache(c, n) for c, n in zip(caches, news)]


def _combine_proj_body(o0, o1, o2, l0, l1, l2, x_ref, w_ref, out_ref):
    a, b, c = l0[...], l1[...], l2[...]
    m = jnp.maximum(jnp.maximum(a, b), c)
    ea, eb, ec = jnp.exp(a - m), jnp.exp(b - m), jnp.exp(c - m)
    mixed = (ea * o0[...] + eb * o1[...] + ec * o2[...]) / (ea + eb + ec)
    out_ref[...] = x_ref[...] + _dot(mixed.astype(BF16), w_ref[...])


def _combine_proj(outs, lses, x2d, w, tm):
    n = x2d.shape[0]
    full = pl.BlockSpec((tm, D_MODEL), lambda i: (i, 0))
    return pl.pallas_call(
        _combine_proj_body, grid=(n // tm,), in_specs=[full] * 7 + [pl.BlockSpec(w.shape, lambda i: (0, 0))],
        out_specs=full, out_shape=jax.ShapeDtypeStruct((n, D_MODEL), F32),
        compiler_params=_cparams("parallel"), name="combine_proj")(*outs, *lses, x2d, w)


def _top_rows(s, k, with_rank=False):
    rank = lax.broadcasted_iota(jnp.int32, (k, s.shape[1]), 0)
    top = jnp.zeros((k, s.shape[1]), F32)
    pos = jnp.full(s.shape, float(k), F32)
    for i in range(k):
        m = jnp.max(s, axis=0, keepdims=True)
        top = jnp.where(rank == i, m, top)
        hit = s == m
        if with_rank:
            pos = jnp.where(hit, float(i), pos)
        s = jnp.where(hit, NEG, s)
    return (top, pos) if with_rank else top


def _kth_largest(cand, k):
    m = None
    for _ in range(k):
        m = jnp.max(cand, axis=0, keepdims=True)
        cand = jnp.where(cand == m, NEG, cand)
    return m


def _pair_sums(a, b):
    parts = [a[0:1, :] + b]
    for i in range(1, PEER_TOPK):
        parts.append(a[i:i + 1, :] + b[0:SUBLANES, :])
    return jnp.concatenate(parts, axis=0)


def _route_body(x_ref, g_ref, wqt_ref, keys_ref, hb_ref, cnt_ref, ea_ref, rank_ref, eb_ref, qt_ref):
    hb = _rms(x_ref[...], g_ref[...]).astype(BF16)
    hb_ref[...] = hb
    qt_ref[...] = _dot_nt(wqt_ref[...], hb).astype(BF16)

    def head(h, carry):
        r1 = pl.multiple_of(h * 2 * PEER_HALF, 2 * PEER_HALF)
        r2 = pl.multiple_of(h * 2 * PEER_HALF + PEER_HALF, PEER_HALF)
        s1 = _dot(keys_ref[2 * h], qt_ref[pl.ds(r1, PEER_HALF), :])
        s2 = _dot(keys_ref[2 * h + 1], qt_ref[pl.ds(r2, PEER_HALF), :])
        a = _top_rows(s1, PEER_TOPK)
        b, rank_b = _top_rows(s2, PEER_TOPK, with_rank=True)
        cand = _pair_sums(a, b)
        tau = _kth_largest(cand, PEER_TOPK)
        mx = a[0:1, :] + b[0:1, :]
        z = jnp.sum(jnp.where(cand >= tau, jnp.exp(cand - mx), 0.0), axis=0, keepdims=True)
        cnt = jnp.zeros(s1.shape, F32)
        for jj in range(PEER_TOPK):
            cnt = cnt + jnp.where(s1 + b[jj:jj + 1, :] >= tau, 1.0, 0.0)
        cnt_ref[h] = cnt
        ea_ref[h] = jnp.exp(s1 - a[0:1, :]) / z
        rank_ref[h] = pltpu.bitcast(rank_b.astype(BF16), jnp.uint32)
        eb_ref[h] = pltpu.bitcast(jnp.exp(s2 - b[0:1, :]).astype(BF16), jnp.uint32)
        return carry

    lax.fori_loop(0, PEER_HEADS, head, 0)


def _route(x2d, gain, wqt, keys, tm):
    n = x2d.shape[0]
    tab = pl.BlockSpec((PEER_HEADS, N_KEYS, tm), lambda i: (0, 0, i))
    packed = pl.BlockSpec((PEER_HEADS, N_KEYS // 2, tm), lambda i: (0, 0, i))
    return pl.pallas_call(
        _route_body, grid=(n // tm,),
        in_specs=[pl.BlockSpec((tm, D_MODEL), lambda i: (i, 0)), pl.BlockSpec((1, D_MODEL), lambda i: (0, 0)),
                  pl.BlockSpec(wqt.shape, lambda i: (0, 0)), pl.BlockSpec(keys.shape, lambda i: (0, 0, 0))],
        out_specs=[pl.BlockSpec((tm, D_MODEL), lambda i: (i, 0)), tab, tab, packed, packed],
        out_shape=[jax.ShapeDtypeStruct((n, D_MODEL), BF16),
                   jax.ShapeDtypeStruct((PEER_HEADS, N_KEYS, n), F32),
                   jax.ShapeDtypeStruct((PEER_HEADS, N_KEYS, n), F32),
                   jax.ShapeDtypeStruct((PEER_HEADS, N_KEYS // 2, n), jnp.uint32),
                   jax.ShapeDtypeStruct((PEER_HEADS, N_KEYS // 2, n), jnp.uint32)],
        scratch_shapes=[pltpu.VMEM((D_MODEL, tm), BF16)],
        compiler_params=_cparams("parallel"), name="peer_route")(x2d, gain.reshape(1, D_MODEL), wqt, keys)


def _gelu(x):
    return 0.5 * x * (1.0 + lax.erf(x * (2.0 ** -0.5)))


def _peer_body(hb_ref, u_ref, vt_ref, cnt_ref, ea_ref, rank_ref, eb_ref, x_ref, o_ref,
               sc_a, sc_b, act_a, act_b, acc_ref, *, tm, te, n_j, n_tiles):
    s = pl.program_id(0)
    j_prev = lax.rem(jnp.clip(s - 2, 0, n_tiles - 1), n_j)

    @pl.when(s == 0)
    def _():
        for ref in (sc_a, sc_b, act_a, act_b):
            ref[...] = jnp.zeros_like(ref)

    @pl.when(j_prev == 0)
    def _():
        acc_ref[...] = jnp.zeros_like(acc_ref)

    tg = min(tm, 2 * LANES)
    zero = jnp.zeros((N_KEYS, LANES), BF16)

    def gate_unit(sc_cur, act_cur, e, c0):
        rows = slice(e * N_KEYS, (e + 1) * N_KEYS)
        prow = slice(e * N_KEYS // 2, (e + 1) * N_KEYS // 2)
        cs = slice(c0, c0 + LANES)
        g = zero
        for h in range(PEER_HEADS):
            cnt = cnt_ref[h, e:e + 1, cs].astype(BF16)
            ea = ea_ref[h, e:e + 1, cs].astype(BF16)
            rank = pltpu.bitcast(rank_ref[h, :, cs], BF16)
            eb = pltpu.bitcast(eb_ref[h, :, cs], BF16)
            g = g + jnp.where(rank < cnt, eb, zero) * ea
        act = _gelu(sc_cur[rows, cs]).astype(BF16) * g
        act_cur[prow, cs] = pltpu.bitcast(act, jnp.uint32)

    def stages(sc_new, sc_cur, act_cur, act_old):
        units = [(e, c0) for e in range(te // N_KEYS) for c0 in range(0, tm, LANES)]
        kb = 2 * LANES
        mb = 4 * LANES
        pieces = [(t0, k0, m0) for t0 in range(0, tm, tg) for k0 in range(0, D_MODEL, kb)
                  for m0 in range(0, D_MODEL, mb)]
        per_piece = len(units) // len(pieces)
        for p, (t0, k0, m0) in enumerate(pieces):
            ts = slice(t0, t0 + tg)
            ks = slice(k0, k0 + kb)
            ms = slice(m0, m0 + mb)
            pks = slice(k0 // 2, (k0 + kb) // 2)
            acc_ref[ms, ts] += _dot(vt_ref[ms, ks], pltpu.bitcast(act_old[pks, ts], BF16))
            part = _dot_nt(u_ref[ms, ks], hb_ref[ts, ks])
            if k0 == 0:
                sc_new[ms, ts] = part
            else:
                sc_new[ms, ts] += part
            for unit in units[p * per_piece:(p + 1) * per_piece]:
                gate_unit(sc_cur, act_cur, *unit)

    @pl.when(lax.rem(s, 2) == 0)
    def _():
        stages(sc_a, sc_b, act_b, act_a)

    @pl.when(lax.rem(s, 2) == 1)
    def _():
        stages(sc_b, sc_a, act_a, act_b)

    @pl.when((s >= 2) & (j_prev == n_j - 1))
    def _():
        o_ref[...] = x_ref[...] + acc_ref[...].T


def _peer_dense(hb, cnt, ea, rank, eb, x2d, u_bf, vt_bf, tm, te):
    n = x2d.shape[0]
    n_j = u_bf.shape[0] // te
    n_tiles = (n // tm) * n_j
    nxt = lambda s: jnp.minimum(s, n_tiles - 1)
    cur = lambda s: jnp.clip(s - 1, 0, n_tiles - 1)
    prv = lambda s: jnp.clip(s - 2, 0, n_tiles - 1)
    first = pl.BlockSpec((PEER_HEADS, te // N_KEYS, tm), lambda s: (0, cur(s) % n_j, cur(s) // n_j))
    second = pl.BlockSpec((PEER_HEADS, N_KEYS // 2, tm), lambda s: (0, 0, cur(s) // n_j))
    return pl.pallas_call(
        functools.partial(_peer_body, tm=tm, te=te, n_j=n_j, n_tiles=n_tiles), grid=(n_tiles + 2,),
        in_specs=[pl.BlockSpec((tm, D_MODEL), lambda s: (nxt(s) // n_j, 0)),
                  pl.BlockSpec((te, D_MODEL), lambda s: (nxt(s) % n_j, 0)),
                  pl.BlockSpec((D_MODEL, te), lambda s: (0, prv(s) % n_j)),
                  first, first, second, second,
                  pl.BlockSpec((tm, D_MODEL), lambda s: (prv(s) // n_j, 0))],
        out_specs=pl.BlockSpec((tm, D_MODEL), lambda s: (prv(s) // n_j, 0)),
        out_shape=jax.ShapeDtypeStruct((n, D_MODEL), F32),
        scratch_shapes=[pltpu.VMEM((te, tm), F32), pltpu.VMEM((te, tm), F32),
                        pltpu.VMEM((te // 2, tm), jnp.uint32), pltpu.VMEM((te // 2, tm), jnp.uint32),
                        pltpu.VMEM((D_MODEL, tm), F32)],
        compiler_params=_cparams("arbitrary"), name="peer_dense")(
            hb, u_bf, vt_bf, cnt, ea, rank, eb, x2d)


def _ple_body(x_ref, p_ref, g_ref, wg_ref, wp_ref, gf_ref, o_ref, *, final_norm):
    x = x_ref[...]
    gate = _sigmoid(_dot(_rms(x, g_ref[...]).astype(BF16), wg_ref[...]))
    emb = _dot(p_ref[...].astype(BF16), wp_ref[...])
    y = x + gate * emb
    if final_norm:
        y = _rms(y, gf_ref[...])
    o_ref[...] = y


def _ple(x2d, p2d, gain, wg, wp, gain_final, final_norm, tm):
    n = x2d.shape[0]
    full = pl.BlockSpec((tm, D_MODEL), lambda i: (i, 0))
    vec = pl.BlockSpec((1, D_MODEL), lambda i: (0, 0))
    return pl.pallas_call(
        functools.partial(_ple_body, final_norm=final_norm), grid=(n // tm,),
        in_specs=[full, pl.BlockSpec((tm, PLE_DIM), lambda i: (i, 0)), vec,
                  pl.BlockSpec(wg.shape, lambda i: (0, 0)), pl.BlockSpec(wp.shape, lambda i: (0, 0)), vec],
        out_specs=full, out_shape=jax.ShapeDtypeStruct((n, D_MODEL), F32),
        compiler_params=_cparams("parallel"), name="ple")(
            x2d, p2d, gain.reshape(1, D_MODEL), wg, wp, gain_final.reshape(1, D_MODEL))


PEER_TE = 1024


def _tail(x2d, p2d, norm_ffn, wqt, keys, u_bf, vt_bf, norm_ple, wg, wp, norm_final, final_norm, tm_route, tm):
    hb, cnt, ea, rank, eb = _route(x2d, norm_ffn, wqt, keys, tm_route)
    x2 = _peer_dense(hb, cnt, ea, rank, eb, x2d, u_bf, vt_bf, tm, PEER_TE)
    return _ple(x2, p2d, norm_ple, wg, wp, norm_final, final_norm, tm)


def kernel(x_prompt, x_sample, state_conv_a, state_conv_qkv, state_delta, cache_kv_w128, cache_kv_w512,
           cache_kv_w2048, p_prompt, p_sample, norm_mix, norm_ffn, norm_ple, norm_final, w_in_ab, conv_a_w,
           conv_a_b, ln_a_g, ln_a_b, conv_qkv_w, a_log, dt_bias, gdn_norm_g, w_out_ab, w_in_c, w_out_c,
           peer_wq, peer_keys, peer_u, peer_v, ple_gate, ple_proj):
    bp, s, _ = x_prompt.shape
    bs, t_new, _ = x_sample.shape
    n_p = bp * s
    n_s = bs * t_new
    nhd = C_HEADS * C_HD
    tm_p = 512
    tm_route_p = 256

    o1 = 2 * D_A
    o2 = o1 + QKV_W
    o3 = o2 + GDN_HEADS * GDN_DV
    w_ab = w_in_ab[0].astype(BF16)
    w_bg = jnp.pad(w_ab[:, o3:], ((0, 0), (0, LANES - 2 * GDN_HEADS)))
    w_ab_parts = [w_ab[:, :o1], w_ab[:, o1:o2], w_ab[:, o2:o3], w_bg]
    w_out_a = w_out_ab[0, :D_A].astype(BF16)
    w_out_b = w_out_ab[0, D_A:].astype(BF16)
    w_c = w_in_c[0].astype(BF16)
    n_g = len(C_PATTERNS)
    w_q_all = w_c[:, :n_g * nhd]
    w_kv = [jnp.concatenate([w_c[:, (n_g + g) * nhd:(n_g + g + 1) * nhd],
                             w_c[:, (2 * n_g + g) * nhd:(2 * n_g + g + 1) * nhd]], axis=1) for g in range(n_g)]
    w_oc = w_out_c[0].astype(BF16)
    wqt = [peer_wq[i].T.astype(BF16) for i in range(2)]
    keys = [peer_keys[i].reshape(2 * PEER_HEADS, N_KEYS, PEER_HALF).astype(BF16) for i in range(2)]
    u_bf = [peer_u[i].astype(BF16) for i in range(2)]
    vt_bf = [peer_v[i].astype(BF16).T for i in range(2)]
    wg = [ple_gate[i].astype(BF16) for i in range(2)]
    wp = [ple_proj[i].astype(BF16) for i in range(2)]

    t_pad = GDN_CHUNK
    xs_pad = jnp.pad(x_sample, ((0, 0), (0, t_pad - t_new), (0, 0)))
    hist_a_s = jnp.pad(state_conv_a[0], ((0, 0), (CONV_HIST - (CONV_A_WIDTH - 1), 0), (0, 0)))
    hist_q_s = jnp.pad(state_conv_qkv[0], ((0, 0), (SUBLANES - (GDN_CONV - 1), 0), (0, 0)))

    def layer0(x3d, hist_a, hist_q, s0, t_valid, ts):
        b, t, _ = x3d.shape
        x2d = x3d.reshape(b * t, D_MODEL)
        zglu, zqkv, zgate, zbg = _norm_mm(x2d, norm_mix[0], w_ab_parts, 512)
        ya, glu = _conv_a(zglu.reshape(b, t, 2 * D_A), hist_a, conv_a_w[0], conv_a_b[0], ln_a_g[0], ln_a_b[0], ts)
        zqkv3 = zqkv.reshape(b, t, QKV_W)
        od, s_fin = _gdn(zqkv3, zgate.reshape(b, t, -1), zbg.reshape(b, t, LANES), hist_q, s0, conv_qkv_w[0],
                         a_log[0], dt_bias[0], gdn_norm_g[0], t_valid)
        x1 = _out_proj_ab(ya.reshape(b * t, D_A), od.reshape(b * t, -1), x2d, w_out_a, w_out_b, 512)
        return x1.reshape(b, t, D_MODEL), glu, zqkv3, s_fin

    xp1, glu_p, zqkv_p, sfin_p = layer0(
        x_prompt, jnp.zeros((bp, CONV_HIST, D_A), F32), jnp.zeros((bp, SUBLANES, QKV_W), F32),
        jnp.zeros((bp, GDN_HEADS, GDN_DK, GDN_DV), F32), s, 256)
    xs1, glu_s, zqkv_s, sfin_s = layer0(xs_pad, hist_a_s, hist_q_s, state_delta[0], t_new, t_pad)
    xs1 = xs1[:, :t_new]

    na = CONV_A_WIDTH - 1
    nq = GDN_CONV - 1
    new_a_p = glu_p[:, s - na:][None]
    new_q_p = zqkv_p[:, s - nq:][None]
    new_a_s = jnp.concatenate([state_conv_a[0], glu_s[:, :t_new]], axis=1)[:, t_new:][None]
    new_q_s = jnp.concatenate([state_conv_qkv[0], zqkv_s[:, :t_new]], axis=1)[:, t_new:][None]

    xp2 = _tail(xp1.reshape(n_p, D_MODEL), p_prompt[0].reshape(n_p, PLE_DIM), norm_ffn[0], wqt[0], keys[0], u_bf[0],
                vt_bf[0], norm_ple[0], wg[0], wp[0], norm_final, False, tm_route_p, tm_p)
    xs2 = _tail(xs1.reshape(n_s, D_MODEL), p_sample[0].reshape(n_s, PLE_DIM), norm_ffn[0], wqt[0], keys[0], u_bf[0],
                vt_bf[0], norm_ple[0], wg[0], wp[0], norm_final, False, n_s, n_s)

    tab_p = _rope_tables(jnp.arange(s))
    tab_s = _rope_tables(PAST_LEN + (jnp.arange(n_s) % t_new))
    q_p = _norm_mm_rope(xp2, norm_mix[1], tab_p, w_q_all, n_g * C_HEADS, tm_p)
    q_s = _norm_mm_rope(xs2, norm_mix[1], tab_s, w_q_all, n_g * C_HEADS, n_s)
    caches = (cache_kv_w128[0], cache_kv_w512[0], cache_kv_w2048[0])
    q_s5 = q_s.reshape(bs, t_new, n_g, C_HEADS, C_HD)
    outs_p, lses_p, outs_s, lses_s, kv_p, kv_new = [], [], [], [], [], []
    for g, (win, dil) in enumerate(C_PATTERNS):
        kvp = _norm_mm_rope(xp2, norm_mix[1], tab_p, w_kv[g], C_HEADS, tm_p)
        o, l = _band_attn(q_p, kvp, g, dil, bp, s)
        outs_p.append(o)
        lses_p.append(l)
        keep = min(win, s)
        kv_p.append(kvp.reshape(bp, s, 2, C_HEADS, C_HD)[:, s - keep:][None])

        kvs = _norm_mm_rope(xs2, norm_mix[1], tab_s, w_kv[g], C_HEADS, n_s).reshape(bs, t_new, 2, C_HEADS, C_HD)
        o, l = _gather_attn(q_s5[:, :, g], caches[g], kvs, dil)
        outs_s.append(o.reshape(n_s, nhd))
        lses_s.append(l.reshape(n_s, nhd))
        kv_new.append(kvs)
    kv_s = [c[None] for c in _roll_caches(caches, kv_new)]

    xp3 = _combine_proj(outs_p, lses_p, xp2, w_oc, tm_p)
    xs3 = _combine_proj(outs_s, lses_s, xs2, w_oc, n_s)
    y_p = _tail(xp3, p_prompt[1].reshape(n_p, PLE_DIM), norm_ffn[1], wqt[1], keys[1], u_bf[1], vt_bf[1],
                norm_ple[1], wg[1], wp[1], norm_final, True, tm_route_p, tm_p)
    y_s = _tail(xs3, p_sample[1].reshape(n_s, PLE_DIM), norm_ffn[1], wqt[1], keys[1], u_bf[1], vt_bf[1],
                norm_ple[1], wg[1], wp[1], norm_final, True, n_s, n_s)

    return (y_p.reshape(bp, s, D_MODEL), y_s.reshape(bs, t_new, D_MODEL),
            new_a_p, new_q_p, sfin_p[None], kv_p[0], kv_p[1], kv_p[2],
            new_a_s, new_q_s, sfin_s[None], kv_s[0], kv_s[1], kv_s[2])
```

```python
import functools
import math

import jax
import jax.numpy as jnp
import numpy as np
from jax import lax
from jax.experimental import pallas as pl
from jax.experimental.pallas import tpu as pltpu

F32 = jnp.float32
BF16 = jnp.bfloat16

D_MODEL = 1024
EPS = 1e-6
D_A = 512
CONV_A_WIDTH = 31
GDN_HEADS = 4
GDN_DK = 128
GDN_DV = 128
GDN_CONV = 4
GDN_CHUNK = 64
QKV_W = GDN_HEADS * (2 * GDN_DK + GDN_DV)
C_PATTERNS = ((128, 1), (512, 4), (2048, 16))
C_HEADS = 8
C_HD = 128
ROT_DIM = 32
ROPE_THETA = 500000.0
ATT_BLOCK = 128
PAST_LEN = 8192
PEER_HEADS = 8
N_KEYS = 128
PEER_TOPK = 16
PEER_HALF = 64
PLE_DIM = 256

LANES = 128
SUBLANES = 8
VMEM_LIMIT_BYTES = 56 * 1024 * 1024

NEG = -1e30
LOG2E = 1.4426950408889634


def _cparams(*sem):
    return pltpu.CompilerParams(dimension_semantics=sem, vmem_limit_bytes=VMEM_LIMIT_BYTES)


def _rms(x, g):
    return x * lax.rsqrt(jnp.mean(x * x, axis=-1, keepdims=True) + EPS) * g


def _sigmoid(x):
    return 1.0 / (1.0 + jnp.exp(-x))


def _dot(a, b):
    return jnp.dot(a, b, preferred_element_type=F32)


def _dot_nt(a, b):
    return lax.dot_general(a, b, (((1,), (1,)), ((), ())), preferred_element_type=F32)


def _dot_tn(a, b):
    return lax.dot_general(a, b, (((0,), (0,)), ((), ())), preferred_element_type=F32)


def _norm_mm_body(x_ref, g_ref, *refs):
    nw = len(refs) // 2
    h = _rms(x_ref[...], g_ref[...]).astype(BF16)
    for w_ref, o_ref in zip(refs[:nw], refs[nw:]):
        o_ref[...] = _dot(h, w_ref[...])


def _norm_mm(x2d, gain, ws, tm):
    n = x2d.shape[0]
    in_specs = [pl.BlockSpec((tm, D_MODEL), lambda i: (i, 0)), pl.BlockSpec((1, D_MODEL), lambda i: (0, 0))]
    in_specs += [pl.BlockSpec(w.shape, lambda i: (0, 0)) for w in ws]
    out_specs = [pl.BlockSpec((tm, w.shape[1]), lambda i: (i, 0)) for w in ws]
    out_shape = [jax.ShapeDtypeStruct((n, w.shape[1]), F32) for w in ws]
    return pl.pallas_call(
        _norm_mm_body, grid=(n // tm,), in_specs=in_specs, out_specs=out_specs, out_shape=out_shape,
        compiler_params=_cparams("parallel"), name="norm_mm")(x2d, gain.reshape(1, D_MODEL), *ws)


def _norm_mm_rope_body(x_ref, g_ref, cos_ref, sa_ref, sb_ref, w_ref, o_ref, *, n_rope, n_heads):
    h = _rms(x_ref[...], g_ref[...]).astype(BF16)
    cos = cos_ref[...]
    sa = sa_ref[...]
    sb = sb_ref[...]
    group = 2 * LANES
    for c0 in range(0, n_heads * C_HD, group):
        og = _dot(h, w_ref[:, c0:c0 + group])
        for hh in range(c0 // C_HD, (c0 + group) // C_HD):
            o = og[:, hh * C_HD - c0:(hh + 1) * C_HD - c0]
            if hh < n_rope:
                half = ROT_DIM // 2
                o = o * cos + pltpu.roll(o, half, 1) * sa + pltpu.roll(o, C_HD - half, 1) * sb
            o_ref[:, hh * C_HD:(hh + 1) * C_HD] = o


def _rope_tables(pos):
    half = ROT_DIM // 2
    inv = 1.0 / (ROPE_THETA ** (jnp.arange(0, ROT_DIM, 2, dtype=F32) / ROT_DIM))
    ang = pos.astype(F32)[:, None] * inv[None, :]
    c, s = jnp.cos(ang), jnp.sin(ang)
    p = pos.shape[0]
    ones = jnp.ones((p, C_HD - ROT_DIM), F32)
    zeros = jnp.zeros((p, C_HD - ROT_DIM), F32)
    zh = jnp.zeros((p, half), F32)
    cos = jnp.concatenate([c, c, ones], axis=1)
    sa = jnp.concatenate([zh, s, zeros], axis=1)
    sb = jnp.concatenate([-s, zh, zeros], axis=1)
    return cos, sa, sb


def _norm_mm_rope(x2d, gain, tables, w, n_rope, tm):
    n = x2d.shape[0]
    ncol = w.shape[1]
    cos, sa, sb = tables
    pblk = cos.shape[0] // tm
    tspec = pl.BlockSpec((tm, C_HD), lambda i: (i % pblk, 0))
    return pl.pallas_call(
        functools.partial(_norm_mm_rope_body, n_rope=n_rope, n_heads=ncol // C_HD),
        grid=(n // tm,),
        in_specs=[pl.BlockSpec((tm, D_MODEL), lambda i: (i, 0)), pl.BlockSpec((1, D_MODEL), lambda i: (0, 0)),
                  tspec, tspec, tspec, pl.BlockSpec(w.shape, lambda i: (0, 0))],
        out_specs=pl.BlockSpec((tm, ncol), lambda i: (i, 0)),
        out_shape=jax.ShapeDtypeStruct((n, ncol), F32),
        compiler_params=_cparams("parallel"), name="norm_mm_rope")(x2d, gain.reshape(1, D_MODEL), cos, sa, sb, w)


CONV_HIST = 32


def _conv_a_body(z_ref, hist_ref, w_ref, b_ref, lg_ref, lb_ref, ya_ref, glu_ref, ext_ref, *, ts):
    t = pl.program_id(1)

    @pl.when(t == 0)
    def _():
        ext_ref[0:CONV_HIST, :] = hist_ref[0]

    @pl.when(t > 0)
    def _():
        ext_ref[0:CONV_HIST, :] = ext_ref[ts:ts + CONV_HIST, :]

    z = z_ref[0]
    glu = z[:, :D_A] * _sigmoid(z[:, D_A:])
    glu_ref[0] = glu
    ext_ref[CONV_HIST:CONV_HIST + ts, :] = glu
    ch = min(ts, 64)
    first = CONV_HIST - (CONV_A_WIDTH - 1)
    for c in range(ts // ch):
        acc = jnp.zeros((ch, D_A), F32)
        for j in range(CONV_A_WIDTH):
            r0 = first + j + c * ch
            acc = acc + w_ref[j:j + 1, :] * ext_ref[r0:r0 + ch, :]
        y = acc + b_ref[...]
        yc = y - jnp.mean(y, axis=-1, keepdims=True)
        ln = yc * lax.rsqrt(jnp.mean(yc * yc, axis=-1, keepdims=True) + EPS) * lg_ref[...] + lb_ref[...]
        ya_ref[0, c * ch:(c + 1) * ch, :] = ln * _sigmoid(ln)


def _conv_a(zglu, hist, conv_w, conv_b, ln_g, ln_b, ts):
    b, t, _ = zglu.shape
    row = lambda a: a.reshape(1, D_A)
    wpad = jnp.pad(conv_w, ((0, CONV_HIST - CONV_A_WIDTH), (0, 0)))
    vec = pl.BlockSpec((1, D_A), lambda i, j: (0, 0))
    return pl.pallas_call(
        functools.partial(_conv_a_body, ts=ts),
        grid=(b, t // ts),
        in_specs=[pl.BlockSpec((1, ts, 2 * D_A), lambda i, j: (i, j, 0)),
                  pl.BlockSpec((1, CONV_HIST, D_A), lambda i, j: (i, 0, 0)),
                  pl.BlockSpec((CONV_HIST, D_A), lambda i, j: (0, 0)), vec, vec, vec],
        out_specs=[pl.BlockSpec((1, ts, D_A), lambda i, j: (i, j, 0)),
                   pl.BlockSpec((1, ts, D_A), lambda i, j: (i, j, 0))],
        out_shape=[jax.ShapeDtypeStruct((b, t, D_A), F32), jax.ShapeDtypeStruct((b, t, D_A), F32)],
        scratch_shapes=[pltpu.VMEM((CONV_HIST + ts, D_A), F32)],
        compiler_params=_cparams("arbitrary", "arbitrary"), name="conv_a")(
            zglu, hist, wpad, row(conv_b), row(ln_g), row(ln_b))


def _gdn_body(zq_ref, zg_ref, zb_ref, hist_ref, s0_ref, cw_ref, alog_ref, dtb_ref, ng_ref,
              od_ref, s_ref, ext_ref, *, t_valid):
    c = pl.program_id(1)
    ck = GDN_CHUNK

    @pl.when(c == 0)
    def _():
        ext_ref[0:SUBLANES, :] = hist_ref[0]
        s_ref[0] = s0_ref[0]

    @pl.when(c > 0)
    def _():
        ext_ref[0:SUBLANES, :] = ext_ref[ck:ck + SUBLANES, :]

    ext_ref[SUBLANES:SUBLANES + ck, :] = zq_ref[0]
    conv = jnp.zeros((ck, QKV_W), F32)
    for j in range(GDN_CONV):
        r0 = SUBLANES - (GDN_CONV - 1) + j
        conv = conv + cw_ref[j:j + 1, :] * ext_ref[r0:r0 + ck, :]
    qkv = conv * _sigmoid(conv)

    zb = zb_ref[0]
    row = c * ck + lax.broadcasted_iota(jnp.int32, (ck, LANES), 0)
    valid = row < t_valid
    beta_all = jnp.where(valid, _sigmoid(zb), 0.0)
    xg = zb + dtb_ref[...]
    softplus = jnp.maximum(xg, 0.0) + jnp.log(1.0 + jnp.exp(-jnp.abs(xg)))
    g_all = jnp.where(valid, -jnp.exp(alog_ref[...]) * softplus, 0.0)

    ri = lax.broadcasted_iota(jnp.int32, (ck, ck), 0)
    ci = lax.broadcasted_iota(jnp.int32, (ck, ck), 1)
    causal = ri >= ci
    strict = ri > ci
    tri = causal.astype(F32)
    eye = (ri == ci).astype(F32)
    ri2 = lax.broadcasted_iota(jnp.int32, (ck, 2 * ck), 0)
    ci2 = lax.broadcasted_iota(jnp.int32, (ck, 2 * ck), 1)
    upper2 = ((ri2 > ci2) | (ci2 == ck)).astype(F32)

    nq = GDN_HEADS * GDN_DK
    heads = range(GDN_HEADS)
    q = [qkv[:, h * GDN_DK:(h + 1) * GDN_DK] for h in heads]
    k = [qkv[:, nq + h * GDN_DK:nq + (h + 1) * GDN_DK] for h in heads]
    v = [qkv[:, 2 * nq + h * GDN_DV:2 * nq + (h + 1) * GDN_DV] for h in heads]
    q = [x * lax.rsqrt(jnp.sum(x * x, axis=-1, keepdims=True) + EPS) * (GDN_DK ** -0.5) for x in q]
    k = [x * lax.rsqrt(jnp.sum(x * x, axis=-1, keepdims=True) + EPS) for x in k]
    beta = [beta_all[:, h:h + 1] for h in heads]
    g = [g_all[:, GDN_HEADS + h:GDN_HEADS + h + 1] for h in heads]
    dm = [jnp.dot(tri, g[h] * upper2, preferred_element_type=F32, precision=lax.Precision.HIGHEST) for h in heads]
    gcum = [x[:, ck:ck + 1] for x in dm]
    glast = [x[ck - 1:ck, :] for x in gcum]
    decay = [jnp.where(causal, jnp.exp(x[:, :ck]), 0.0) for x in dm]
    egc = [jnp.exp(x) for x in gcum]
    kb = [k[h] * beta[h] for h in heads]
    p = [-jnp.where(strict, _dot_nt(kb[h], k[h]) * decay[h], 0.0) for h in heads]
    tinv = [eye + x for x in p]
    for _ in range(int(math.log2(ck)) - 1):
        p = [_dot(x, x) for x in p]
        tinv = [tinv[h] + _dot(tinv[h], p[h]) for h in heads]
    u = [_dot(tinv[h], v[h] * beta[h]) for h in heads]
    w = [_dot(tinv[h], kb[h] * egc[h]) for h in heads]
    intra = [jnp.where(causal, _dot_nt(q[h], k[h]) * decay[h], 0.0) for h in heads]
    s_old = [s_ref[0, h] for h in heads]
    vnew = [u[h] - _dot(w[h], s_old[h]) for h in heads]
    o = [_dot(q[h] * egc[h], s_old[h]) + _dot(intra[h], vnew[h]) for h in heads]
    for h in heads:
        s_ref[0, h] = s_old[h] * jnp.exp(glast[h]) + _dot_tn(k[h] * jnp.exp(glast[h] - gcum[h]), vnew[h])
    for h in heads:
        gate = zg_ref[0, :, h * GDN_DV:(h + 1) * GDN_DV]
        on = o[h] * lax.rsqrt(jnp.mean(o[h] * o[h], axis=-1, keepdims=True) + EPS) * ng_ref[...]
        od_ref[0, :, h * GDN_DV:(h + 1) * GDN_DV] = on * (gate * _sigmoid(gate))


def _gdn(zqkv, zgate, zbg, hist, s0, conv_w, a_log, dt_bias, norm_g, t_valid):
    b, t, _ = zqkv.shape
    ck = GDN_CHUNK
    lane_vec = lambda a: jnp.zeros((1, LANES), F32).at[0, GDN_HEADS:2 * GDN_HEADS].set(a)
    return pl.pallas_call(
        functools.partial(_gdn_body, t_valid=t_valid),
        grid=(b, t // ck),
        in_specs=[pl.BlockSpec((1, ck, QKV_W), lambda i, j: (i, j, 0)),
                  pl.BlockSpec((1, ck, GDN_HEADS * GDN_DV), lambda i, j: (i, j, 0)),
                  pl.BlockSpec((1, ck, LANES), lambda i, j: (i, j, 0)),
                  pl.BlockSpec((1, SUBLANES, QKV_W), lambda i, j: (i, 0, 0)),
                  pl.BlockSpec((1, GDN_HEADS, GDN_DK, GDN_DV), lambda i, j: (i, 0, 0, 0)),
                  pl.BlockSpec((GDN_CONV, QKV_W), lambda i, j: (0, 0)),
                  pl.BlockSpec((1, LANES), lambda i, j: (0, 0)),
                  pl.BlockSpec((1, LANES), lambda i, j: (0, 0)),
                  pl.BlockSpec((1, GDN_DV), lambda i, j: (0, 0))],
        out_specs=[pl.BlockSpec((1, ck, GDN_HEADS * GDN_DV), lambda i, j: (i, j, 0)),
                   pl.BlockSpec((1, GDN_HEADS, GDN_DK, GDN_DV), lambda i, j: (i, 0, 0, 0))],
        out_shape=[jax.ShapeDtypeStruct((b, t, GDN_HEADS * GDN_DV), F32),
                   jax.ShapeDtypeStruct((b, GDN_HEADS, GDN_DK, GDN_DV), F32)],
        scratch_shapes=[pltpu.VMEM((SUBLANES + ck, QKV_W), F32)],
        compiler_params=_cparams("arbitrary", "arbitrary"), name="gdn")(
            zqkv, zgate, zbg, hist, s0, conv_w, lane_vec(a_log), lane_vec(dt_bias), norm_g.reshape(1, GDN_DV))


def _out_proj_ab_body(ya_ref, od_ref, x_ref, wa_ref, wb_ref, o_ref):
    y = _dot(ya_ref[...].astype(BF16), wa_ref[...]) + _dot(od_ref[...].astype(BF16), wb_ref[...])
    o_ref[...] = x_ref[...] + y


def _out_proj_ab(ya, od, x2d, wa, wb, tm):
    n = x2d.shape[0]
    half = pl.BlockSpec((tm, D_A), lambda i: (i, 0))
    full = pl.BlockSpec((tm, D_MODEL), lambda i: (i, 0))
    wspec = pl.BlockSpec((D_A, D_MODEL), lambda i: (0, 0))
    return pl.pallas_call(
        _out_proj_ab_body, grid=(n // tm,), in_specs=[half, half, full, wspec, wspec], out_specs=full,
        out_shape=jax.ShapeDtypeStruct((n, D_MODEL), F32), compiler_params=_cparams("parallel"),
        name="out_proj_ab")(ya, od, x2d, wa, wb)


def _band_streams(blk, q, kc, vc, kp, vp):
    n = range(len(q))
    ri = lax.broadcasted_iota(jnp.int32, (ATT_BLOCK, ATT_BLOCK), 0)
    ci = lax.broadcasted_iota(jnp.int32, (ATT_BLOCK, ATT_BLOCK), 1)
    cur_ok = ci <= ri
    prev_ok = (ci >= ri) & (blk > 0)
    scale = C_HD ** -0.5
    s_c = [jnp.where(cur_ok, _dot_nt(q[i], kc[i]) * scale, NEG) for i in n]
    m = [jnp.max(x, axis=-1, keepdims=True) for x in s_c]
    if kp is not None:
        s_p = [jnp.where(prev_ok, _dot_nt(q[i], kp[i]) * scale, NEG) for i in n]
        m = [jnp.maximum(m[i], jnp.max(s_p[i], axis=-1, keepdims=True)) for i in n]
    p_c = [jnp.exp(s_c[i] - m[i]) for i in n]
    den = [jnp.sum(x, axis=-1, keepdims=True) for x in p_c]
    o = [_dot(p_c[i].astype(BF16), vc[i]) for i in n]
    if kp is not None:
        p_p = [jnp.exp(s_p[i] - m[i]) for i in n]
        den = [den[i] + jnp.sum(p_p[i], axis=-1, keepdims=True) for i in n]
        o = [o[i] + _dot(p_p[i].astype(BF16), vp[i]) for i in n]
    outs = [o[i] / den[i] for i in n]
    lses = [jnp.broadcast_to(m[i] + jnp.log(den[i]), (ATT_BLOCK, C_HD)) for i in n]
    return outs, lses


def _band_attn_body(q_ref, kc_ref, kp_ref, o_ref, l_ref):
    nhd = C_HEADS * C_HD
    cols = [slice(h * C_HD, (h + 1) * C_HD) for h in range(C_HEADS)]
    vcols = [slice(nhd + h * C_HD, nhd + (h + 1) * C_HD) for h in range(C_HEADS)]
    outs, lses = _band_streams(
        pl.program_id(1),
        [q_ref[0, :, c].astype(BF16) for c in cols],
        [kc_ref[0, :, c].astype(BF16) for c in cols], [kc_ref[0, :, c].astype(BF16) for c in vcols],
        [kp_ref[0, :, c].astype(BF16) for c in cols], [kp_ref[0, :, c].astype(BF16) for c in vcols])
    for h, c in enumerate(cols):
        o_ref[0, :, c] = outs[h]
        l_ref[0, :, c] = lses[h]


BAND_LOCKSTEP = 4


def _band_attn_dilated_body(q_ref, k_ref, v_ref, kp_ref, vp_ref, o_ref, l_ref, *, dil, has_prev):
    blk = pl.program_id(1)

    def stream_group(gidx, carry):
        rows = [pl.ds(gidx * BAND_LOCKSTEP + i, ATT_BLOCK, stride=dil) for i in range(BAND_LOCKSTEP)]
        load = lambda ref: [ref[0, r, :].astype(BF16) for r in rows]
        outs, lses = _band_streams(blk, load(q_ref), load(k_ref), load(v_ref),
                                   load(kp_ref) if has_prev else None, load(vp_ref) if has_prev else None)
        for i, r in enumerate(rows):
            o_ref[0, r, :] = outs[i]
            l_ref[0, r, :] = lses[i]
        return carry

    lax.fori_loop(0, dil // BAND_LOCKSTEP, stream_group, 0)


def _band_attn(q_all, kv, gi, dil, b, s):
    nhd = C_HEADS * C_HD
    n_groups = len(C_PATTERNS)
    q3 = q_all.reshape(b, s, n_groups * nhd)
    kv3 = kv.reshape(b, s, 2 * nhd)
    if dil == 1:
        nblk = s // ATT_BLOCK
        ospec = pl.BlockSpec((1, ATT_BLOCK, nhd), lambda i, j: (i, j, 0))
        o, l = pl.pallas_call(
            _band_attn_body, grid=(b, nblk),
            in_specs=[pl.BlockSpec((1, ATT_BLOCK, nhd), lambda i, j: (i, j, gi)),
                      pl.BlockSpec((1, ATT_BLOCK, 2 * nhd), lambda i, j: (i, j, 0)),
                      pl.BlockSpec((1, ATT_BLOCK, 2 * nhd), lambda i, j: (i, jnp.maximum(j - 1, 0), 0))],
            out_specs=[ospec, ospec],
            out_shape=[jax.ShapeDtypeStruct((b, s, nhd), F32)] * 2,
            compiler_params=_cparams("parallel", "arbitrary"), name="band_attn")(q3, kv3, kv3)
    else:
        rows = dil * ATT_BLOCK
        nblk = s // rows
        cur = lambda c0: pl.BlockSpec((1, rows, C_HD), lambda i, j, h: (i, j, c0 + h))
        prev = lambda c0: pl.BlockSpec((1, rows, C_HD), lambda i, j, h: (i, jnp.maximum(j - 1, 0), c0 + h))
        o, l = pl.pallas_call(
            functools.partial(_band_attn_dilated_body, dil=dil, has_prev=nblk > 1), grid=(b, nblk, C_HEADS),
            in_specs=[cur(gi * C_HEADS), cur(0), cur(C_HEADS), prev(0), prev(C_HEADS)],
            out_specs=[cur(0), cur(0)],
            out_shape=[jax.ShapeDtypeStruct((b, s, nhd), F32)] * 2,
            compiler_params=_cparams("parallel", "arbitrary", "arbitrary"), name="band_attn_dilated")(
                q3, kv3, kv3, kv3, kv3)
    return o.reshape(b * s, nhd), l.reshape(b * s, nhd)


def _gather_attn_body(q_ref, c_ref, n_ref, ones_ref, o_ref, l_ref, *, dil, t_new):
    scale = C_HD ** -0.5
    rows = c_ref.shape[1]
    crow = lax.broadcasted_iota(jnp.int32, (rows, C_HEADS, C_HD), 0)
    nrow = lax.broadcasted_iota(jnp.int32, (t_new, C_HEADS, C_HD), 0)

    def head_dots(k, q):
        prod = (k * q[None]).astype(BF16).reshape(k.shape[0] * C_HEADS, C_HD)
        return (_dot(prod, ones_ref[...]) * scale).reshape(k.shape)

    for t in range(t_new):
        r = 0 if dil == 1 else t
        q = q_ref[0, t]
        s_c = head_dots(c_ref[0, :, r, 0], q)
        s_n = head_dots(n_ref[0, :, 0], q)
        if dil == 1:
            s_c = jnp.where(crow >= t, s_c, NEG)
            s_n = jnp.where(nrow <= t, s_n, NEG)
        else:
            s_n = jnp.where(nrow == t, s_n, NEG)
        m = jnp.maximum(jnp.max(s_c, axis=0), jnp.max(s_n, axis=0))
        p_c = jnp.exp(s_c - m[None])
        p_n = jnp.exp(s_n - m[None])
        den = jnp.sum(p_c, axis=0) + jnp.sum(p_n, axis=0)
        o = jnp.sum(p_c * c_ref[0, :, r, 1], axis=0) + jnp.sum(p_n * n_ref[0, :, 1], axis=0)
        o_ref[0, t] = o / den
        l_ref[0, t] = m + jnp.log(den)


def _gather_attn(q_g, cache, kv_new, dil):
    b, t_new = q_g.shape[:2]
    L = cache.shape[1]
    rows = L // dil
    streams = 1 if dil == 1 else t_new
    cv = cache.reshape(b, rows, dil, 2, C_HEADS, C_HD)
    ones = jnp.ones((C_HD, C_HD), BF16)
    ospec = pl.BlockSpec((1, t_new, C_HEADS, C_HD), lambda i: (i, 0, 0, 0))
    return pl.pallas_call(
        functools.partial(_gather_attn_body, dil=dil, t_new=t_new), grid=(b,),
        in_specs=[ospec,
                  pl.BlockSpec((1, rows, streams, 2, C_HEADS, C_HD), lambda i: (i, 0, 0, 0, 0, 0)),
                  pl.BlockSpec((1, t_new, 2, C_HEADS, C_HD), lambda i: (i, 0, 0, 0, 0)),
                  pl.BlockSpec(ones.shape, lambda i: (0, 0))],
        out_specs=[ospec, ospec],
        out_shape=[jax.ShapeDtypeStruct((b, t_new, C_HEADS, C_HD), F32)] * 2,
        compiler_params=_cparams("parallel"), name="gather_attn")(q_g, cv, kv_new, ones)


ROLL_ROWS = 256


def _roll_body(cur_ref, nxt_ref, new_ref, o_ref, *, t_new):
    j = pl.program_id(1)
    rows = cur_ref.shape[1]
    o_ref[0, 0:rows - t_new] = cur_ref[0, t_new:rows]

    @pl.when(j < pl.num_programs(1) - 1)
    def _():
        o_ref[0, rows - t_new:rows] = nxt_ref[0]

    @pl.when(j == pl.num_programs(1) - 1)
    def _():
        o_ref[0, rows - t_new:rows] = new_ref[0]


def _roll_cache(cache, new):
    b, L = cache.shape[:2]
    t_new = new.shape[1]
    rows = min(L, ROLL_ROWS)
    per = rows // t_new
    last = L // t_new - 1
    tail = cache.shape[2:]
    zeros = (0,) * len(tail)
    return pl.pallas_call(
        functools.partial(_roll_body, t_new=t_new), grid=(b, L // rows),
        in_specs=[pl.BlockSpec((1, rows) + tail, lambda i, j: (i, j) + zeros),
                  pl.BlockSpec((1, t_new) + tail, lambda i, j: (i, jnp.minimum((j + 1) * per, last)) + zeros),
                  pl.BlockSpec((1, t_new) + tail, lambda i, j: (i, 0) + zeros)],
        out_specs=pl.BlockSpec((1, rows) + tail, lambda i, j: (i, j) + zeros),
        out_shape=jax.ShapeDtypeStruct(cache.shape, cache.dtype),
        compiler_params=_cparams("parallel", "arbitrary"), name="roll_cache")(cache, cache, new)


def _roll_caches(caches, news):
    return [_roll_cache(c, n) for c, n in zip(caches, news)]


def _combine_proj_body(o0, o1, o2, l0, l1, l2, x_ref, w_ref, out_ref):
    a, b, c = l0[...], l1[...], l2[...]
    m = jnp.maximum(jnp.maximum(a, b), c)
    ea, eb, ec = jnp.exp(a - m), jnp.exp(b - m), jnp.exp(c - m)
    mixed = (ea * o0[...] + eb * o1[...] + ec * o2[...]) / (ea + eb + ec)
    out_ref[...] = x_ref[...] + _dot(mixed.astype(BF16), w_ref[...])


def _combine_proj(outs, lses, x2d, w, tm):
    n = x2d.shape[0]
    full = pl.BlockSpec((tm, D_MODEL), lambda i: (i, 0))
    return pl.pallas_call(
        _combine_proj_body, grid=(n // tm,), in_specs=[full] * 7 + [pl.BlockSpec(w.shape, lambda i: (0, 0))],
        out_specs=full, out_shape=jax.ShapeDtypeStruct((n, D_MODEL), F32),
        compiler_params=_cparams("parallel"), name="combine_proj")(*outs, *lses, x2d, w)


def _top_rows(s, k, with_rank=False):
    rank = lax.broadcasted_iota(jnp.int32, (k, s.shape[1]), 0)
    top = jnp.zeros((k, s.shape[1]), F32)
    pos = jnp.full(s.shape, float(k), F32)
    for i in range(k):
        m = jnp.max(s, axis=0, keepdims=True)
        top = jnp.where(rank == i, m, top)
        hit = s == m
        if with_rank:
            pos = jnp.where(hit, float(i), pos)
        s = jnp.where(hit, NEG, s)
    return (top, pos) if with_rank else top


def _kth_largest(cand, k):
    m = None
    for _ in range(k):
        m = jnp.max(cand, axis=0, keepdims=True)
        cand = jnp.where(cand == m, NEG, cand)
    return m


def _pair_sums(a, b):
    parts = [a[0:1, :] + b]
    for i in range(1, PEER_TOPK):
        parts.append(a[i:i + 1, :] + b[0:SUBLANES, :])
    return jnp.concatenate(parts, axis=0)


def _route_body(x_ref, g_ref, wqt_ref, keys_ref, hb_ref, cnt_ref, ea_ref, rank_ref, eb_ref, qt_ref):
    hb = _rms(x_ref[...], g_ref[...]).astype(BF16)
    hb_ref[...] = hb
    qt_ref[...] = _dot_nt(wqt_ref[...], hb).astype(BF16)

    def head(h, carry):
        r1 = pl.multiple_of(h * 2 * PEER_HALF, 2 * PEER_HALF)
        r2 = pl.multiple_of(h * 2 * PEER_HALF + PEER_HALF, PEER_HALF)
        s1 = _dot(keys_ref[2 * h], qt_ref[pl.ds(r1, PEER_HALF), :])
        s2 = _dot(keys_ref[2 * h + 1], qt_ref[pl.ds(r2, PEER_HALF), :])
        a = _top_rows(s1, PEER_TOPK)
        b, rank_b = _top_rows(s2, PEER_TOPK, with_rank=True)
        cand = _pair_sums(a, b)
        tau = _kth_largest(cand, PEER_TOPK)
        mx = a[0:1, :] + b[0:1, :]
        z = jnp.sum(jnp.where(cand >= tau, jnp.exp(cand - mx), 0.0), axis=0, keepdims=True)
        cnt = jnp.zeros(s1.shape, F32)
        for jj in range(PEER_TOPK):
            cnt = cnt + jnp.where(s1 + b[jj:jj + 1, :] >= tau, 1.0, 0.0)
        cnt_ref[h] = cnt
        ea_ref[h] = jnp.exp(s1 - a[0:1, :]) / z
        rank_p = pltpu.bitcast(rank_b.astype(BF16), jnp.uint32)
        eb_p = pltpu.bitcast(jnp.exp(s2 - b[0:1, :]).astype(BF16), jnp.uint32)
        for c in range(rank_ref.shape[0]):
            rank_ref[c, h] = rank_p[:, c * LANES:(c + 1) * LANES]
            eb_ref[c, h] = eb_p[:, c * LANES:(c + 1) * LANES]
        return carry

    lax.fori_loop(0, PEER_HEADS, head, 0)


def _route(x2d, gain, wqt, keys, tm):
    n = x2d.shape[0]
    tab = pl.BlockSpec((PEER_HEADS, N_KEYS, tm), lambda i: (0, 0, i))
    packed = pl.BlockSpec((tm // LANES, PEER_HEADS, N_KEYS // 2, LANES), lambda i: (i, 0, 0, 0))
    return pl.pallas_call(
        _route_body, grid=(n // tm,),
        in_specs=[pl.BlockSpec((tm, D_MODEL), lambda i: (i, 0)), pl.BlockSpec((1, D_MODEL), lambda i: (0, 0)),
                  pl.BlockSpec(wqt.shape, lambda i: (0, 0)), pl.BlockSpec(keys.shape, lambda i: (0, 0, 0))],
        out_specs=[pl.BlockSpec((tm, D_MODEL), lambda i: (i, 0)), tab, tab, packed, packed],
        out_shape=[jax.ShapeDtypeStruct((n, D_MODEL), BF16),
                   jax.ShapeDtypeStruct((PEER_HEADS, N_KEYS, n), F32),
                   jax.ShapeDtypeStruct((PEER_HEADS, N_KEYS, n), F32),
                   jax.ShapeDtypeStruct((n // LANES, PEER_HEADS, N_KEYS // 2, LANES), jnp.uint32),
                   jax.ShapeDtypeStruct((n // LANES, PEER_HEADS, N_KEYS // 2, LANES), jnp.uint32)],
        scratch_shapes=[pltpu.VMEM((D_MODEL, tm), BF16)],
        compiler_params=_cparams("parallel"), name="peer_route")(x2d, gain.reshape(1, D_MODEL), wqt, keys)


def _gelu(x):
    return 0.5 * x * (1.0 + lax.erf(x * (2.0 ** -0.5)))


MXU_TILE = 256
ACC_ROWS = 512
ACC_ENTRIES = ACC_ROWS // 4


def _peer_body(hb_ref, u_ref, vt_ref, cnt_ref, ea_ref, rank_ref, eb_ref, x_ref, o_ref,
               sc_a, sc_b, act_a, act_b, acc_ref, *, tm, te, n_j, n_tiles):
    s = pl.program_id(0)
    j_prev = lax.rem(jnp.clip(s - 2, 0, n_tiles - 1), n_j)

    @pl.when(s == 0)
    def _():
        for ref in (sc_a, sc_b, act_a, act_b):
            ref[...] = jnp.zeros_like(ref)

    @pl.when(j_prev == 0)
    def _():
        acc_ref[...] = jnp.zeros_like(acc_ref)

    zero = jnp.zeros((N_KEYS, LANES), BF16)

    def gate_unit(sc_cur, act_cur, e, c):
        rows = slice(e * N_KEYS, (e + 1) * N_KEYS)
        prow = slice(e * N_KEYS // 2, (e + 1) * N_KEYS // 2)
        cs = slice(c * LANES, (c + 1) * LANES)
        g = zero
        for h in range(PEER_HEADS):
            cnt = cnt_ref[h, e:e + 1, cs].astype(BF16)
            ea = ea_ref[h, e:e + 1, cs].astype(BF16)
            rank = pltpu.bitcast(rank_ref[c, h], BF16)
            eb = pltpu.bitcast(eb_ref[c, h], BF16)
            g = g + jnp.where(rank < cnt, eb, zero) * ea
        act = _gelu(sc_cur[c, rows, :]).astype(BF16) * g
        act_cur[c, prow, :] = pltpu.bitcast(act, jnp.uint32)

    def stages(sc_new, sc_cur, act_cur, act_old):
        groups = MXU_TILE // LANES
        units = [(e, c) for e in range(te // N_KEYS) for c in range(tm // LANES)]
        products = [(t0, m0) for t0 in range(0, tm, MXU_TILE) for m0 in range(0, te, ACC_ROWS)]
        n_pass = te // MXU_TILE
        per_pass = len(units) // (len(products) * n_pass)
        done = 0

        def drain(addr, t0, m0):
            ms = slice(m0, m0 + ACC_ROWS)
            val = pltpu.matmul_pop(addr, (ACC_ROWS, MXU_TILE), F32, mxu_index=0)
            pre = pltpu.matmul_pop(addr, (ACC_ROWS, MXU_TILE), F32, mxu_index=1)
            for i in range(groups):
                c = t0 // LANES + i
                acc_ref[c, ms, :] += val[:, i * LANES:(i + 1) * LANES]
                sc_new[c, ms, :] = pre[:, i * LANES:(i + 1) * LANES]

        pending = None
        for pi, (t0, m0) in enumerate(products):
            addr = (pi % 2) * ACC_ENTRIES
            ts, ms = slice(t0, t0 + MXU_TILE), slice(m0, m0 + ACC_ROWS)
            for k in range(n_pass):
                ks = slice(k * MXU_TILE, (k + 1) * MXU_TILE)
                pks = slice(k * MXU_TILE // 2, (k + 1) * MXU_TILE // 2)
                reg = k % 2
                act_tile = jnp.concatenate(
                    [pltpu.bitcast(act_old[t0 // LANES + i, pks, :], BF16) for i in range(groups)], axis=1)
                pltpu.matmul_push_rhs(act_tile, staging_register=reg, mxu_index=0)
                pltpu.matmul_push_rhs(hb_ref[ts, ks], staging_register=reg, mxu_index=1, transpose=True)
                pltpu.matmul_acc_lhs(addr, vt_ref[ms, ks], mxu_index=0, load_staged_rhs=reg)
                pltpu.matmul_acc_lhs(addr, u_ref[ms, ks], mxu_index=1, load_staged_rhs=reg)
                if k == 0 and pending is not None:
                    drain(*pending)
                for unit in units[done:done + per_pass]:
                    gate_unit(sc_cur, act_cur, *unit)
                done += per_pass
            pending = (addr, t0, m0)
        drain(*pending)

    @pl.when(lax.rem(s, 2) == 0)
    def _():
        stages(sc_a, sc_b, act_b, act_a)

    @pl.when(lax.rem(s, 2) == 1)
    def _():
        stages(sc_b, sc_a, act_a, act_b)

    @pl.when((s >= 2) & (j_prev == n_j - 1))
    def _():
        for c in range(tm // LANES):
            cs = slice(c * LANES, (c + 1) * LANES)
            o_ref[cs, :] = x_ref[cs, :] + acc_ref[c].T


def _peer_dense(hb, cnt, ea, rank, eb, x2d, u_bf, vt_bf, tm, te):
    n = x2d.shape[0]
    assert te == D_MODEL and tm % MXU_TILE == 0 and n % tm == 0
    n_j = u_bf.shape[0] // te
    n_tiles = (n // tm) * n_j
    ng = tm // LANES
    nxt = lambda s: jnp.minimum(s, n_tiles - 1)
    cur = lambda s: jnp.clip(s - 1, 0, n_tiles - 1)
    prv = lambda s: jnp.clip(s - 2, 0, n_tiles - 1)
    first = pl.BlockSpec((PEER_HEADS, te // N_KEYS, tm), lambda s: (0, cur(s) % n_j, cur(s) // n_j))
    second = pl.BlockSpec((ng, PEER_HEADS, N_KEYS // 2, LANES), lambda s: (cur(s) // n_j, 0, 0, 0))
    return pl.pallas_call(
        functools.partial(_peer_body, tm=tm, te=te, n_j=n_j, n_tiles=n_tiles), grid=(n_tiles + 2,),
        in_specs=[pl.BlockSpec((tm, D_MODEL), lambda s: (nxt(s) // n_j, 0)),
                  pl.BlockSpec((te, D_MODEL), lambda s: (nxt(s) % n_j, 0)),
                  pl.BlockSpec((D_MODEL, te), lambda s: (0, prv(s) % n_j)),
                  first, first, second, second,
                  pl.BlockSpec((tm, D_MODEL), lambda s: (prv(s) // n_j, 0))],
        out_specs=pl.BlockSpec((tm, D_MODEL), lambda s: (prv(s) // n_j, 0)),
        out_shape=jax.ShapeDtypeStruct((n, D_MODEL), F32),
        scratch_shapes=[pltpu.VMEM((ng, te, LANES), F32), pltpu.VMEM((ng, te, LANES), F32),
                        pltpu.VMEM((ng, te // 2, LANES), jnp.uint32), pltpu.VMEM((ng, te // 2, LANES), jnp.uint32),
                        pltpu.VMEM((ng, D_MODEL, LANES), F32)],
        compiler_params=_cparams("arbitrary"), name="peer_dense")(
            hb, u_bf, vt_bf, cnt, ea, rank, eb, x2d)


def _ple_body(x_ref, p_ref, g_ref, wg_ref, wp_ref, gf_ref, o_ref, *, final_norm):
    x = x_ref[...]
    gate = _sigmoid(_dot(_rms(x, g_ref[...]).astype(BF16), wg_ref[...]))
    emb = _dot(p_ref[...].astype(BF16), wp_ref[...])
    y = x + gate * emb
    if final_norm:
        y = _rms(y, gf_ref[...])
    o_ref[...] = y


def _ple(x2d, p2d, gain, wg, wp, gain_final, final_norm, tm):
    n = x2d.shape[0]
    full = pl.BlockSpec((tm, D_MODEL), lambda i: (i, 0))
    vec = pl.BlockSpec((1, D_MODEL), lambda i: (0, 0))
    return pl.pallas_call(
        functools.partial(_ple_body, final_norm=final_norm), grid=(n // tm,),
        in_specs=[full, pl.BlockSpec((tm, PLE_DIM), lambda i: (i, 0)), vec,
                  pl.BlockSpec(wg.shape, lambda i: (0, 0)), pl.BlockSpec(wp.shape, lambda i: (0, 0)), vec],
        out_specs=full, out_shape=jax.ShapeDtypeStruct((n, D_MODEL), F32),
        compiler_params=_cparams("parallel"), name="ple")(
            x2d, p2d, gain.reshape(1, D_MODEL), wg, wp, gain_final.reshape(1, D_MODEL))


PEER_TE = 1024


def _tail(x2d, p2d, norm_ffn, wqt, keys, u_bf, vt_bf, norm_ple, wg, wp, norm_final, final_norm, tm_route, tm):
    hb, cnt, ea, rank, eb = _route(x2d, norm_ffn, wqt, keys, tm_route)
    x2 = _peer_dense(hb, cnt, ea, rank, eb, x2d, u_bf, vt_bf, tm, PEER_TE)
    return _ple(x2, p2d, norm_ple, wg, wp, norm_final, final_norm, tm)


def kernel(x_prompt, x_sample, state_conv_a, state_conv_qkv, state_delta, cache_kv_w128, cache_kv_w512,
           cache_kv_w2048, p_prompt, p_sample, norm_mix, norm_ffn, norm_ple, norm_final, w_in_ab, conv_a_w,
           conv_a_b, ln_a_g, ln_a_b, conv_qkv_w, a_log, dt_bias, gdn_norm_g, w_out_ab, w_in_c, w_out_c,
           peer_wq, peer_keys, peer_u, peer_v, ple_gate, ple_proj):
    bp, s, _ = x_prompt.shape
    bs, t_new, _ = x_sample.shape
    n_p = bp * s
    n_s = bs * t_new
    nhd = C_HEADS * C_HD
    tm_p = 512
    tm_route_p = 256

    o1 = 2 * D_A
    o2 = o1 + QKV_W
    o3 = o2 + GDN_HEADS * GDN_DV
    w_ab = w_in_ab[0].astype(BF16)
    w_bg = jnp.pad(w_ab[:, o3:], ((0, 0), (0, LANES - 2 * GDN_HEADS)))
    w_ab_parts = [w_ab[:, :o1], w_ab[:, o1:o2], w_ab[:, o2:o3], w_bg]
    w_out_a = w_out_ab[0, :D_A].astype(BF16)
    w_out_b = w_out_ab[0, D_A:].astype(BF16)
    w_c = w_in_c[0].astype(BF16)
    n_g = len(C_PATTERNS)
    w_q_all = w_c[:, :n_g * nhd]
    w_kv = [jnp.concatenate([w_c[:, (n_g + g) * nhd:(n_g + g + 1) * nhd],
                             w_c[:, (2 * n_g + g) * nhd:(2 * n_g + g + 1) * nhd]], axis=1) for g in range(n_g)]
    w_oc = w_out_c[0].astype(BF16)
    wqt = [peer_wq[i].T.astype(BF16) for i in range(2)]
    keys = [peer_keys[i].reshape(2 * PEER_HEADS, N_KEYS, PEER_HALF).astype(BF16) for i in range(2)]
    u_bf = [peer_u[i].astype(BF16) for i in range(2)]
    vt_bf = [peer_v[i].astype(BF16).T for i in range(2)]
    wg = [ple_gate[i].astype(BF16) for i in range(2)]
    wp = [ple_proj[i].astype(BF16) for i in range(2)]

    t_pad = GDN_CHUNK
    xs_pad = jnp.pad(x_sample, ((0, 0), (0, t_pad - t_new), (0, 0)))
    hist_a_s = jnp.pad(state_conv_a[0], ((0, 0), (CONV_HIST - (CONV_A_WIDTH - 1), 0), (0, 0)))
    hist_q_s = jnp.pad(state_conv_qkv[0], ((0, 0), (SUBLANES - (GDN_CONV - 1), 0), (0, 0)))

    def layer0(x3d, hist_a, hist_q, s0, t_valid, ts):
        b, t, _ = x3d.shape
        x2d = x3d.reshape(b * t, D_MODEL)
        zglu, zqkv, zgate, zbg = _norm_mm(x2d, norm_mix[0], w_ab_parts, 512)
        ya, glu = _conv_a(zglu.reshape(b, t, 2 * D_A), hist_a, conv_a_w[0], conv_a_b[0], ln_a_g[0], ln_a_b[0], ts)
        zqkv3 = zqkv.reshape(b, t, QKV_W)
        od, s_fin = _gdn(zqkv3, zgate.reshape(b, t, -1), zbg.reshape(b, t, LANES), hist_q, s0, conv_qkv_w[0],
                         a_log[0], dt_bias[0], gdn_norm_g[0], t_valid)
        x1 = _out_proj_ab(ya.reshape(b * t, D_A), od.reshape(b * t, -1), x2d, w_out_a, w_out_b, 512)
        return x1.reshape(b, t, D_MODEL), glu, zqkv3, s_fin

    xp1, glu_p, zqkv_p, sfin_p = layer0(
        x_prompt, jnp.zeros((bp, CONV_HIST, D_A), F32), jnp.zeros((bp, SUBLANES, QKV_W), F32),
        jnp.zeros((bp, GDN_HEADS, GDN_DK, GDN_DV), F32), s, 256)
    xs1, glu_s, zqkv_s, sfin_s = layer0(xs_pad, hist_a_s, hist_q_s, state_delta[0], t_new, t_pad)
    xs1 = xs1[:, :t_new]

    na = CONV_A_WIDTH - 1
    nq = GDN_CONV - 1
    new_a_p = glu_p[:, s - na:][None]
    new_q_p = zqkv_p[:, s - nq:][None]
    new_a_s = jnp.concatenate([state_conv_a[0], glu_s[:, :t_new]], axis=1)[:, t_new:][None]
    new_q_s = jnp.concatenate([state_conv_qkv[0], zqkv_s[:, :t_new]], axis=1)[:, t_new:][None]

    xp2 = _tail(xp1.reshape(n_p, D_MODEL), p_prompt[0].reshape(n_p, PLE_DIM), norm_ffn[0], wqt[0], keys[0], u_bf[0],
                vt_bf[0], norm_ple[0], wg[0], wp[0], norm_final, False, tm_route_p, tm_p)
    n_sp = -(-n_s // MXU_TILE) * MXU_TILE
    pad_s = lambda a: jnp.pad(a, ((0, n_sp - n_s), (0, 0)))
    xs2 = _tail(pad_s(xs1.reshape(n_s, D_MODEL)), pad_s(p_sample[0].reshape(n_s, PLE_DIM)), norm_ffn[0], wqt[0],
                keys[0], u_bf[0], vt_bf[0], norm_ple[0], wg[0], wp[0], norm_final, False, n_sp, n_sp)[:n_s]

    tab_p = _rope_tables(jnp.arange(s))
    tab_s = _rope_tables(PAST_LEN + (jnp.arange(n_s) % t_new))
    q_p = _norm_mm_rope(xp2, norm_mix[1], tab_p, w_q_all, n_g * C_HEADS, tm_p)
    q_s = _norm_mm_rope(xs2, norm_mix[1], tab_s, w_q_all, n_g * C_HEADS, n_s)
    caches = (cache_kv_w128[0], cache_kv_w512[0], cache_kv_w2048[0])
    q_s5 = q_s.reshape(bs, t_new, n_g, C_HEADS, C_HD)
    outs_p, lses_p, outs_s, lses_s, kv_p, kv_new = [], [], [], [], [], []
    for g, (win, dil) in enumerate(C_PATTERNS):
        kvp = _norm_mm_rope(xp2, norm_mix[1], tab_p, w_kv[g], C_HEADS, tm_p)
        o, l = _band_attn(q_p, kvp, g, dil, bp, s)
        outs_p.append(o)
        lses_p.append(l)
        keep = min(win, s)
        kv_p.append(kvp.reshape(bp, s, 2, C_HEADS, C_HD)[:, s - keep:][None])

        kvs = _norm_mm_rope(xs2, norm_mix[1], tab_s, w_kv[g], C_HEADS, n_s).reshape(bs, t_new, 2, C_HEADS, C_HD)
        o, l = _gather_attn(q_s5[:, :, g], caches[g], kvs, dil)
        outs_s.append(o.reshape(n_s, nhd))
        lses_s.append(l.reshape(n_s, nhd))
        kv_new.append(kvs)
    kv_s = [c[None] for c in _roll_caches(caches, kv_new)]

    xp3 = _combine_proj(outs_p, lses_p, xp2, w_oc, tm_p)
    xs3 = _combine_proj(outs_s, lses_s, xs2, w_oc, n_s)
    y_p = _tail(xp3, p_prompt[1].reshape(n_p, PLE_DIM), norm_ffn[1], wqt[1], keys[1], u_bf[1], vt_bf[1],
                norm_ple[1], wg[1], wp[1], norm_final, True, tm_route_p, tm_p)
    y_s = _tail(pad_s(xs3), pad_s(p_sample[1].reshape(n_s, PLE_DIM)), norm_ffn[1], wqt[1], keys[1], u_bf[1],
                vt_bf[1], norm_ple[1], wg[1], wp[1], norm_final, True, n_sp, n_sp)[:n_s]

    return (y_p.reshape(bp, s, D_MODEL), y_s.reshape(bs, t_new, D_MODEL),
            new_a_p, new_q_p, sfin_p[None], kv_p[0], kv_p[1], kv_p[2],
            new_a_s, new_q_s, sfin_s[None], kv_s[0], kv_s[1], kv_s[2])
```
